```python
import math, functools
import jax, jax.numpy as jnp
from jax import lax
import numpy as np

D_MODEL = 2048
BATCH = 4
SEQ = 2048
DEPTH = 4
DEC_BATCH = 8
DEC_SEQ = 8
PAST_LEN = 16384
PAGE_SIZE = 128

ATT_HEADS = 8
HEAD_DIM = 128
ATT_WIDTH = ATT_HEADS * HEAD_DIM
Q_BLOCK = 128
SSM_WIDTH = D_MODEL // 4
SSM_GROUP = 16
SSM_GROUPS = SSM_WIDTH // SSM_GROUP
SSM_STATE = 64
POOL_WIDTH = D_MODEL // 4
POOL_WINDOWS = (2, 4, 8, 16)
POOL_GROUPS = len(POOL_WINDOWS)
POOL_GROUP_WIDTH = POOL_WIDTH // POOL_GROUPS
POOL_BUF = max(POOL_WINDOWS) - 1
N_BRANCH = 3
MIX_WIDTH = ATT_WIDTH + SSM_WIDTH + POOL_WIDTH
D_FF = 5632
CONV_WIDTH = 3
CONV_BUF = CONV_WIDTH - 1
N_MOD = 6
EPS = 1e-6
OFF_Q = 0
OFF_K = OFF_Q + ATT_WIDTH
OFF_V = OFF_K + ATT_WIDTH
OFF_F = OFF_V + ATT_WIDTH
OFF_SSM = OFF_F + ATT_HEADS
OFF_POOL = OFF_SSM + SSM_WIDTH
OFF_GATE = OFF_POOL + POOL_WIDTH
IN_WIDTH = OFF_GATE + N_BRANCH * D_MODEL

kernel_name = "fox_s5_pool_gated_parallel_decoder_step"


def rms_norm(x, g):
    xf = x.astype(jnp.float32)
    y = xf * lax.rsqrt(jnp.mean(xf * xf, axis=-1, keepdims=True) + EPS)
    return (y * g.astype(jnp.float32)).astype(x.dtype)


def fox_prompt(q, k, v, logf):
    b, s = q.shape[0], q.shape[1]
    n_blk = s // Q_BLOCK
    scale = HEAD_DIM ** -0.5
    F = jnp.cumsum(logf, axis=1).transpose(0, 2, 1)
    key_pos = jnp.arange(s)
    qb = q.reshape(b, n_blk, Q_BLOCK, ATT_HEADS, HEAD_DIM).transpose(1, 0, 2, 3, 4)
    Fq = F.reshape(b, ATT_HEADS, n_blk, Q_BLOCK).transpose(2, 0, 1, 3)

    def block(args):
        q_blk, f_blk, i = args
        q_pos = i * Q_BLOCK + jnp.arange(Q_BLOCK)
        sc = jnp.einsum('bqhd,bkhd->bhqk', q_blk, k, preferred_element_type=jnp.float32) * scale
        sc = sc + (f_blk[..., :, None] - F[:, :, None, :])
        sc = jnp.where(key_pos[None, :] <= q_pos[:, None], sc, -jnp.inf)
        p = jax.nn.softmax(sc, axis=-1).astype(v.dtype)
        return jnp.einsum('bhqk,bkhd->bqhd', p, v)

    out = lax.map(block, (qb, Fq, jnp.arange(n_blk)))
    return out.transpose(1, 0, 2, 3, 4).reshape(b, s, ATT_WIDTH)


def fox_sample(q, k, v, logf, k_cache, v_cache, lf_cache, page_table, layer):
    db, ds = q.shape[0], q.shape[1]
    past = page_table.shape[1] * PAGE_SIZE
    scale = HEAD_DIM ** -0.5
    k_past = k_cache[layer, page_table].reshape(db, past, ATT_HEADS, HEAD_DIM)
    v_past = v_cache[layer, page_table].reshape(db, past, ATT_HEADS, HEAD_DIM)
    lf_past = lf_cache[layer, page_table].reshape(db, past, ATT_HEADS).astype(jnp.float32)
    k_all = jnp.concatenate([k_past.astype(k.dtype), k], axis=1)
    v_all = jnp.concatenate([v_past.astype(v.dtype), v], axis=1)
    F = jnp.cumsum(jnp.concatenate([lf_past, logf], axis=1), axis=1).transpose(0, 2, 1)
    sc = jnp.einsum('bqhd,bkhd->bhqk', q, k_all, preferred_element_type=jnp.float32) * scale
    sc = sc + (F[:, :, past:, None] - F[:, :, None, :])
    key_pos = jnp.arange(past + ds)
    q_pos = past + jnp.arange(ds)
    sc = jnp.where(key_pos[None, :] <= q_pos[:, None], sc, -jnp.inf)
    p = jax.nn.softmax(sc, axis=-1).astype(v.dtype)
    return jnp.einsum('bhqk,bkhd->bqhd', p, v_all).reshape(db, ds, ATT_WIDTH)


def _complex_scan_combine(e1, e2):
    a1r, a1i, b1r, b1i = e1
    a2r, a2i, b2r, b2i = e2
    return (a2r * a1r - a2i * a1i,
            a2r * a1i + a2i * a1r,
            a2r * b1r - a2i * b1i + b2r,
            a2r * b1i + a2i * b1r + b2i)


def s5_mixer(u, h0_re, h0_im, p):
    bsz, L = u.shape[0], u.shape[1]
    f32 = jnp.float32
    ug = u.astype(f32).reshape(bsz, L, SSM_GROUPS, SSM_GROUP)
    a_re, a_im = p["ssm_a_re"].astype(f32), p["ssm_a_im"].astype(f32)
    step = jnp.exp(p["ssm_log_step"].astype(f32))[:, None]
    mag = jnp.exp(step * a_re)
    ab_re, ab_im = mag * jnp.cos(step * a_im), mag * jnp.sin(step * a_im)
    den = a_re * a_re + a_im * a_im
    z_re = ((ab_re - 1.0) * a_re + ab_im * a_im) / den
    z_im = (ab_im * a_re - (ab_re - 1.0) * a_im) / den
    bu_re = jnp.einsum('blgn,gpn->blgp', ug, p["ssm_b_re"].astype(f32))
    bu_im = jnp.einsum('blgn,gpn->blgp', ug, p["ssm_b_im"].astype(f32))
    in_re = z_re * bu_re - z_im * bu_im
    in_im = z_re * bu_im + z_im * bu_re
    h0r, h0i = h0_re.astype(f32), h0_im.astype(f32)
    in_re = in_re.at[:, 0].add(ab_re * h0r - ab_im * h0i)
    in_im = in_im.at[:, 0].add(ab_re * h0i + ab_im * h0r)
    a_br = jnp.broadcast_to(ab_re, in_re.shape)
    a_bi = jnp.broadcast_to(ab_im, in_im.shape)
    _, _, h_re, h_im = lax.associative_scan(_complex_scan_combine, (a_br, a_bi, in_re, in_im), axis=1)
    y = (jnp.einsum('blgp,gnp->blgn', h_re, p["ssm_c_re"].astype(f32))
         - jnp.einsum('blgp,gnp->blgn', h_im, p["ssm_c_im"].astype(f32)))
    y = y + p["ssm_d"].astype(f32).reshape(SSM_GROUPS, SSM_GROUP) * ug
    y = jax.nn.gelu(y.reshape(bsz, L, SSM_WIDTH))
    y = y * jax.nn.sigmoid(y @ p["w_glu"].astype(f32))
    return y.astype(u.dtype), h_re[:, -1], h_im[:, -1]


def pool_mixer(u, buf, pos0, pool_w, pool_scale):
    bsz, L = u.shape[0], u.shape[1]
    ext = jnp.concatenate([buf.astype(jnp.float32), u.astype(jnp.float32)], axis=1)
    cs = jnp.concatenate([jnp.zeros((bsz, 1, POOL_WIDTH), jnp.float32), jnp.cumsum(ext, axis=1)], axis=1)
    pos = pos0 + jnp.arange(L)
    outs = []
    for gi, w in enumerate(POOL_WINDOWS):
        lo_c, hi_c = gi * POOL_GROUP_WIDTH, (gi + 1) * POOL_GROUP_WIDTH
        hi = cs[:, POOL_BUF + 1:POOL_BUF + 1 + L, lo_c:hi_c]
        lo = cs[:, POOL_BUF + 1 - w:POOL_BUF + 1 - w + L, lo_c:hi_c]
        cnt = jnp.minimum(pos + 1, w).astype(jnp.float32)[None, :, None]
        outs.append((hi - lo) / cnt - ext[:, POOL_BUF:, lo_c:hi_c])
    pooled = jnp.concatenate(outs, axis=-1).reshape(bsz, L, POOL_GROUPS, POOL_GROUP_WIDTH)
    mixed = jnp.einsum('blgc,gcd->blgd', pooled, pool_w.astype(jnp.float32)).reshape(bsz, L, POOL_WIDTH)
    mixed = mixed * pool_scale.astype(jnp.float32)
    return mixed.astype(u.dtype), ext[:, -POOL_BUF:].astype(u.dtype)


def conv_ffn(h, buf, w_up, conv_w, conv_b, w_down):
    L = h.shape[1]
    up = h @ w_up
    ext = jnp.concatenate([buf.astype(up.dtype), up], axis=1)
    conv = conv_b + sum(ext[:, i:i + L] * conv_w[i] for i in range(CONV_WIDTH))
    gate, val = conv[..., :D_FF], conv[..., D_FF:]
    act = jax.nn.gelu(gate) * val
    return act @ w_down, ext[:, -CONV_BUF:]


def trunk_layer(x, c, p, attn_fn, h0_re, h0_im, pool_buf, conv_buf, pos0):
    bsz, L = x.shape[0], x.shape[1]
    mod = c @ p["w_ada"] + p["b_ada"]
    sh_m, sc_m, g_m, sh_f, sc_f, g_f = jnp.split(mod[:, None, :], N_MOD, axis=-1)
    h = rms_norm(x, p["g_pre_mix"]) * (1 + sc_m) + sh_m
    z = h @ p["w_in"]
    q = z[..., OFF_Q:OFF_K].reshape(bsz, L, ATT_HEADS, HEAD_DIM)
    k = z[..., OFF_K:OFF_V].reshape(bsz, L, ATT_HEADS, HEAD_DIM)
    v = z[..., OFF_V:OFF_F].reshape(bsz, L, ATT_HEADS, HEAD_DIM)
    logf = jax.nn.log_sigmoid(z[..., OFF_F:OFF_SSM].astype(jnp.float32) + p["b_f"].astype(jnp.float32))
    o_att = attn_fn(q, k, v, logf)
    o_ssm, h_re, h_im = s5_mixer(z[..., OFF_SSM:OFF_POOL], h0_re, h0_im, p)
    o_pool, new_pool = pool_mixer(z[..., OFF_POOL:OFF_GATE], pool_buf, pos0, p["pool_w"], p["pool_scale"])
    gates = jax.nn.sigmoid(z[..., OFF_GATE:].astype(jnp.float32)).astype(x.dtype)
    gates = gates.reshape(bsz, L, N_BRANCH, D_MODEL)
    wb = p["w_branch"]
    merged = (gates[:, :, 0] * (o_att @ wb[:ATT_WIDTH])
              + gates[:, :, 1] * (o_ssm @ wb[ATT_WIDTH:ATT_WIDTH + SSM_WIDTH])
              + gates[:, :, 2] * (o_pool @ wb[ATT_WIDTH + SSM_WIDTH:]))
    x = x + g_m * rms_norm(merged @ p["w_out"], p["g_post_mix"])
    h = rms_norm(x, p["g_pre_ffn"]) * (1 + sc_f) + sh_f
    f, new_conv = conv_ffn(h, conv_buf, p["w_up"], p["conv_w"], p["conv_b"], p["w_down"])
    x = x + g_f * rms_norm(f, p["g_post_ffn"])
    return x, k, v, logf, h_re, h_im, new_pool, new_conv


def _normal(k, shape, scale):
    return jax.random.normal(k, shape, jnp.float32) * scale


def setup_inputs(seed: int = 0) -> dict:
    key = jax.random.key(seed)
    ks = iter(jax.random.split(key, 48))
    n_pages = PAST_LEN // PAGE_SIZE
    n_pool = DEC_BATCH * n_pages * 5 // 4
    f32 = jnp.float32
    x_prompt = _normal(next(ks), (BATCH, SEQ, D_MODEL), 1.0)
    x_sample = _normal(next(ks), (DEC_BATCH, DEC_SEQ, D_MODEL), 1.0)
    cache_k = _normal(next(ks), (DEPTH, n_pool, PAGE_SIZE, ATT_HEADS, HEAD_DIM), 1.0)
    cache_v = _normal(next(ks), (DEPTH, n_pool, PAGE_SIZE, ATT_HEADS, HEAD_DIM), 1.0)
    cache_logf = jax.nn.log_sigmoid(4.0 + _normal(next(ks), (DEPTH, n_pool, PAGE_SIZE, ATT_HEADS), 1.0))
    perm = jax.random.permutation(next(ks), n_pool)
    page_table = perm[:DEC_BATCH * n_pages].reshape(DEC_BATCH, n_pages).astype(jnp.int32)
    state_ssm_re = _normal(next(ks), (DEPTH, DEC_BATCH, SSM_GROUPS, SSM_STATE), 0.1)
    state_ssm_im = _normal(next(ks), (DEPTH, DEC_BATCH, SSM_GROUPS, SSM_STATE), 0.1)
    state_pool = _normal(next(ks), (DEPTH, DEC_BATCH, POOL_BUF, POOL_WIDTH), 1.0)
    state_ffn_conv = _normal(next(ks), (DEPTH, DEC_BATCH, CONV_BUF, 2 * D_FF), 1.0)
    c_prompt = _normal(next(ks), (BATCH, D_MODEL), 1.0)
    c_sample = _normal(next(ks), (DEC_BATCH, D_MODEL), 1.0)
    w_ada = _normal(next(ks), (DEPTH, D_MODEL, N_MOD * D_MODEL), 0.3 * D_MODEL ** -0.5)
    b_ada = _normal(next(ks), (DEPTH, N_MOD * D_MODEL), 0.01)
    g_pre_mix = 1.0 + _normal(next(ks), (DEPTH, D_MODEL), 0.01)
    g_post_mix = 1.0 + _normal(next(ks), (DEPTH, D_MODEL), 0.01)
    g_pre_ffn = 1.0 + _normal(next(ks), (DEPTH, D_MODEL), 0.01)
    g_post_ffn = 1.0 + _normal(next(ks), (DEPTH, D_MODEL), 0.01)
    w_in = _normal(next(ks), (DEPTH, D_MODEL, IN_WIDTH), D_MODEL ** -0.5)
    b_f = 4.0 + _normal(next(ks), (DEPTH, ATT_HEADS), 0.1)
    ssm_a_re = -0.5 + _normal(next(ks), (DEPTH, SSM_GROUPS, SSM_STATE), 0.01)
    ssm_a_im = (math.pi * jnp.arange(SSM_STATE, dtype=f32))[None, None, :] + _normal(next(ks), (DEPTH, SSM_GROUPS, SSM_STATE), 0.01)
    ssm_log_step = jax.random.uniform(next(ks), (DEPTH, SSM_GROUPS), f32, math.log(1e-3), math.log(1e-1))
    ssm_b_re = _normal(next(ks), (DEPTH, SSM_GROUPS, SSM_STATE, SSM_GROUP), (2 * SSM_GROUP) ** -0.5)
    ssm_b_im = _normal(next(ks), (DEPTH, SSM_GROUPS, SSM_STATE, SSM_GROUP), (2 * SSM_GROUP) ** -0.5)
    ssm_c_re = _normal(next(ks), (DEPTH, SSM_GROUPS, SSM_GROUP, SSM_STATE), (2 * SSM_STATE) ** -0.5)
    ssm_c_im = _normal(next(ks), (DEPTH, SSM_GROUPS, SSM_GROUP, SSM_STATE), (2 * SSM_STATE) ** -0.5)
    ssm_d = _normal(next(ks), (DEPTH, SSM_WIDTH), 1.0)
    w_glu = _normal(next(ks), (DEPTH, SSM_WIDTH, SSM_WIDTH), SSM_WIDTH ** -0.5)
    pool_w = _normal(next(ks), (DEPTH, POOL_GROUPS, POOL_GROUP_WIDTH, POOL_GROUP_WIDTH), POOL_GROUP_WIDTH ** -0.5)
    pool_scale = 1.0 + _normal(next(ks), (DEPTH, POOL_WIDTH), 0.1)
    w_branch = _normal(next(ks), (DEPTH, MIX_WIDTH, D_MODEL), MIX_WIDTH ** -0.5)
    w_out = _normal(next(ks), (DEPTH, D_MODEL, D_MODEL), D_MODEL ** -0.5)
    w_up = _normal(next(ks), (DEPTH, D_MODEL, 2 * D_FF), D_MODEL ** -0.5)
    conv_w = _normal(next(ks), (DEPTH, CONV_WIDTH, 2 * D_FF), CONV_WIDTH ** -0.5)
    conv_b = _normal(next(ks), (DEPTH, 2 * D_FF), 0.01)
    w_down = _normal(next(ks), (DEPTH, D_FF, D_MODEL), D_FF ** -0.5)
    return {"x_prompt": x_prompt, "x_sample": x_sample,
            "cache_k": cache_k, "cache_v": cache_v, "cache_logf": cache_logf,
            "page_table": page_table,
            "state_ssm_re": state_ssm_re, "state_ssm_im": state_ssm_im,
            "state_pool": state_pool, "state_ffn_conv": state_ffn_conv,
            "c_prompt": c_prompt, "c_sample": c_sample,
            "w_ada": w_ada, "b_ada": b_ada,
            "g_pre_mix": g_pre_mix, "g_post_mix": g_post_mix, "g_pre_ffn": g_pre_ffn, "g_post_ffn": g_post_ffn,
            "w_in": w_in, "b_f": b_f,
            "ssm_a_re": ssm_a_re, "ssm_a_im": ssm_a_im, "ssm_log_step": ssm_log_step,
            "ssm_b_re": ssm_b_re, "ssm_b_im": ssm_b_im, "ssm_c_re": ssm_c_re, "ssm_c_im": ssm_c_im,
            "ssm_d": ssm_d, "w_glu": w_glu,
            "pool_w": pool_w, "pool_scale": pool_scale,
            "w_branch": w_branch, "w_out": w_out,
            "w_up": w_up, "conv_w": conv_w, "conv_b": conv_b, "w_down": w_down}


def reference(x_prompt, x_sample, cache_k, cache_v, cache_logf, page_table,
              state_ssm_re, state_ssm_im, state_pool, state_ffn_conv,
              c_prompt, c_sample,
              w_ada, b_ada, g_pre_mix, g_post_mix, g_pre_ffn, g_post_ffn,
              w_in, b_f, ssm_a_re, ssm_a_im, ssm_log_step,
              ssm_b_re, ssm_b_im, ssm_c_re, ssm_c_im, ssm_d, w_glu,
              pool_w, pool_scale, w_branch, w_out, w_up, conv_w, conv_b, w_down):
    past = page_table.shape[1] * PAGE_SIZE
    bp = x_prompt.shape[0]
    xp, xs = x_prompt, x_sample
    kp_l, vp_l, fp_l, srp_l, sip_l, pp_l, cp_l = [], [], [], [], [], [], []
    ks_l, vs_l, fs_l, srs_l, sis_l, ps_l, cs_l = [], [], [], [], [], [], []
    for l in range(DEPTH):
        p = {"w_ada": w_ada[l], "b_ada": b_ada[l],
             "g_pre_mix": g_pre_mix[l], "g_post_mix": g_post_mix[l],
             "g_pre_ffn": g_pre_ffn[l], "g_post_ffn": g_post_ffn[l],
             "w_in": w_in[l], "b_f": b_f[l],
             "ssm_a_re": ssm_a_re[l], "ssm_a_im": ssm_a_im[l], "ssm_log_step": ssm_log_step[l],
             "ssm_b_re": ssm_b_re[l], "ssm_b_im": ssm_b_im[l],
             "ssm_c_re": ssm_c_re[l], "ssm_c_im": ssm_c_im[l],
             "ssm_d": ssm_d[l], "w_glu": w_glu[l],
             "pool_w": pool_w[l], "pool_scale": pool_scale[l],
             "w_branch": w_branch[l], "w_out": w_out[l],
             "w_up": w_up[l], "conv_w": conv_w[l], "conv_b": conv_b[l], "w_down": w_down[l]}
        zr = jnp.zeros((bp, SSM_GROUPS, SSM_STATE), jnp.float32)
        zpool = jnp.zeros((bp, POOL_BUF, POOL_WIDTH), xp.dtype)
        zconv = jnp.zeros((bp, CONV_BUF, 2 * D_FF), xp.dtype)
        xp, k, v, lf, hr, hi, pb, cb = trunk_layer(xp, c_prompt, p, fox_prompt, zr, zr, zpool, zconv, 0)
        kp_l.append(k); vp_l.append(v); fp_l.append(lf)
        srp_l.append(hr); sip_l.append(hi); pp_l.append(pb); cp_l.append(cb)
        attn_s = functools.partial(fox_sample, k_cache=cache_k, v_cache=cache_v, lf_cache=cache_logf,
                                   page_table=page_table, layer=l)
        xs, k, v, lf, hr, hi, pb, cb = trunk_layer(xs, c_sample, p, attn_s, state_ssm_re[l], state_ssm_im[l],
                                                   state_pool[l], state_ffn_conv[l], past)
        ks_l.append(k); vs_l.append(v); fs_l.append(lf)
        srs_l.append(hr); sis_l.append(hi); ps_l.append(pb); cs_l.append(cb)
    st = jnp.stack
    return (xp, xs,
            st(kp_l), st(vp_l), st(fp_l), st(srp_l), st(sip_l), st(pp_l), st(cp_l),
            st(ks_l), st(vs_l), st(fs_l), st(srs_l), st(sis_l), st(ps_l), st(cs_l))
```

```python
import functools
import math

import jax
import jax.numpy as jnp
from jax import lax
from jax.experimental import pallas as pl
from jax.experimental.pallas import tpu as pltpu

F32 = jnp.float32
BF16 = jnp.bfloat16

EPS = 1e-6
POOL_WINDOWS = (2, 4, 8, 16)
POOL_BUF = max(POOL_WINDOWS) - 1
POOL_HDR = POOL_BUF + 1
CONV_WIDTH = 3
CONV_BUF = CONV_WIDTH - 1
CONV_HDR = 16
SUBLANES = 8
LANES = 128
MIB = 1024 * 1024
NEG_BIG = -1e30


def _cparams(semantics, vmem_mib):
    return pltpu.CompilerParams(dimension_semantics=semantics, vmem_limit_bytes=vmem_mib * MIB)


def _resident(shape):
    nd = len(shape)
    return pl.BlockSpec(shape, lambda *_: (0,) * nd, pipeline_mode=pl.Buffered(1))


def _dot(a, b):
    return jnp.dot(a, b, preferred_element_type=F32)


def _dot_nt(a, b):
    return lax.dot_general(a, b, (((1,), (1,)), ((), ())), preferred_element_type=F32)


def _rms(x, g):
    return x * lax.rsqrt(jnp.mean(x * x, axis=-1, keepdims=True) + EPS) * g


def _gelu_tanh(x):
    c = math.sqrt(2.0 / math.pi)
    return 0.5 * x * (1.0 + jnp.tanh(c * (x + 0.044715 * (x * x * x))))


def _log_sigmoid(x):
    return jnp.minimum(x, 0.0) - jnp.log1p(jnp.exp(-jnp.abs(x)))


def _div_pow2(x, c):
    assert c > 0 and c & (c - 1) == 0
    return x >> (c.bit_length() - 1)


def _lane_cumsum(x):
    n = x.shape[-1]
    lane = lax.broadcasted_iota(jnp.int32, x.shape, x.ndim - 1)
    k = 1
    while k < n:
        x = x + jnp.where(lane >= k, pltpu.roll(x, k, axis=x.ndim - 1), 0.0)
        k *= 2
    return x


def _ada_kernel(c_ref, w_ref, b_ref, o_ref):
    o_ref[...] = _dot(c_ref[...], w_ref[...].astype(BF16)) + b_ref[...]


def _ada(c_rows, w_ada, b_ada):
    depth, d, n = w_ada.shape
    r = c_rows.shape[0]
    tn = 1024
    return pl.pallas_call(
        _ada_kernel,
        grid=(depth, n // tn),
        in_specs=[pl.BlockSpec((r, d), lambda l, j: (0, 0)),
                  pl.BlockSpec((None, d, tn), lambda l, j: (l, 0, j)),
                  pl.BlockSpec((None, 1, tn), lambda l, j: (l, 0, j))],
        out_specs=pl.BlockSpec((None, r, tn), lambda l, j: (l, 0, j)),
        out_shape=jax.ShapeDtypeStruct((depth, r, n), F32),
        compiler_params=_cparams(("arbitrary", "arbitrary"), 40),
        name="ada",
    )(c_rows, w_ada, b_ada.reshape(depth, 1, n))


def _ssm_disc_kernel(are_ref, aim_ref, ls_ref, abre_ref, abim_ref, zre_ref, zim_ref):
    a_re, a_im = are_ref[...], aim_ref[...]
    step = jnp.exp(ls_ref[...])
    mag = jnp.exp(step * a_re)
    ab_re = mag * jnp.cos(step * a_im)
    ab_im = mag * jnp.sin(step * a_im)
    den = a_re * a_re + a_im * a_im
    abre_ref[...] = ab_re
    abim_ref[...] = ab_im
    zre_ref[...] = ((ab_re - 1.0) * a_re + ab_im * a_im) / den
    zim_ref[...] = (ab_im * a_re - (ab_re - 1.0) * a_im) / den


def _ssm_disc(a_re, a_im, log_step):
    depth, g, p = a_re.shape
    n = g * p
    ls = jnp.broadcast_to(log_step[:, :, None], (depth, g, p)).reshape(depth, n)
    spec = pl.BlockSpec((depth, n), lambda: (0, 0))
    return pl.pallas_call(
        _ssm_disc_kernel,
        in_specs=[spec] * 3,
        out_specs=[spec] * 4,
        out_shape=[jax.ShapeDtypeStruct((depth, n), F32)] * 4,
        name="ssm_disc",
    )(a_re.reshape(depth, n), a_im.reshape(depth, n), ls)


def _inproj_kernel(x_ref, g_ref, sc_ref, sh_ref, wa_ref, wft_ref, bf_ref,
                   h_ref, q_ref, k_ref, v_ref, us_ref, up_ref, lf_ref, *, aw, sw, pw):
    h = _rms(x_ref[...], g_ref[...]) * (1.0 + sc_ref[...]) + sh_ref[...]
    hb = h.astype(BF16)
    h_ref[...] = hb
    q_ref[...] = _dot(hb, wa_ref[:, 0:aw]).astype(BF16)
    k_ref[...] = _dot(hb, wa_ref[:, aw:2 * aw])
    v_ref[...] = _dot(hb, wa_ref[:, 2 * aw:3 * aw])
    us_ref[...] = _dot(hb, wa_ref[:, 3 * aw:3 * aw + sw])
    up_ref[...] = _dot(hb, wa_ref[:, 3 * aw + sw:3 * aw + sw + pw])
    lf_ref[...] = _log_sigmoid(_dot_nt(wft_ref[...], hb) + bf_ref[...])


def _inproj(x, g, mod, wa, wft, bf, *, tm, seq, nb, aw, sw, pw, time_major):
    t, d = x.shape
    heads = wft.shape[0]
    rb = mod.shape[2]
    if time_major:
        assert tm == t
        bidx = lambda i: 0
        us_shape, us_spec = (t, sw), pl.BlockSpec((tm, sw), lambda i: (i, 0))
    else:
        tpb = seq // tm
        bidx = lambda i: i // tpb
        us_shape, us_spec = (seq, nb * sw), pl.BlockSpec((tm, sw), lambda i: (i % tpb, i // tpb))
    row = lambda w: pl.BlockSpec((tm, w), lambda i: (i, 0))
    modspec = lambda kk: pl.BlockSpec((None, None, rb, d), lambda i: (kk, bidx(i), 0, 0))
    outs = pl.pallas_call(
        functools.partial(_inproj_kernel, aw=aw, sw=sw, pw=pw),
        grid=(t // tm,),
        in_specs=[row(d), _resident((1, d)), modspec(1), modspec(0),
                  _resident(wa.shape), _resident(wft.shape), _resident(bf.shape)],
        out_specs=[row(d), row(aw), row(aw), row(aw), us_spec, row(pw),
                   pl.BlockSpec((heads, tm), lambda i: (0, i))],
        out_shape=[jax.ShapeDtypeStruct((t, d), BF16), jax.ShapeDtypeStruct((t, aw), BF16),
                   jax.ShapeDtypeStruct((t, aw), F32), jax.ShapeDtypeStruct((t, aw), F32),
                   jax.ShapeDtypeStruct(us_shape, F32), jax.ShapeDtypeStruct((t, pw), F32),
                   jax.ShapeDtypeStruct((heads, t), F32)],
        compiler_params=_cparams(("arbitrary",), 48),
        name="inproj",
    )(x, g, mod, mod, wa, wft, bf)
    return outs


def _gates_kernel(h_ref, w_ref, o_ref):
    o_ref[...] = jax.nn.sigmoid(_dot(h_ref[...], w_ref[...]))


def _gates(h, wg, *, tm):
    t, d = h.shape
    n = wg.shape[1]
    tn = 1024
    return pl.pallas_call(
        _gates_kernel,
        grid=(n // tn, t // tm),
        in_specs=[pl.BlockSpec((tm, d), lambda j, i: (i, 0)),
                  pl.BlockSpec((d, tn), lambda j, i: (0, j))],
        out_specs=pl.BlockSpec((tm, tn), lambda j, i: (i, j)),
        out_shape=jax.ShapeDtypeStruct((t, n), F32),
        compiler_params=_cparams(("arbitrary", "arbitrary"), 48),
        name="gates",
    )(h, wg)


def _fcum_kernel(lf_ref, o_ref):
    o_ref[...] = _lane_cumsum(lf_ref[...])


def _fcum(lft, seq):
    h, t = lft.shape
    return pl.pallas_call(
        _fcum_kernel,
        grid=(t // seq,),
        in_specs=[pl.BlockSpec((h, seq), lambda b: (0, b))],
        out_specs=pl.BlockSpec((h, seq), lambda b: (0, b)),
        out_shape=jax.ShapeDtypeStruct((h, t), F32),
        name="fcum",
    )(lft)


def _attn_kernel(q_ref, k_ref, v_ref, f_ref, o_ref, m_sc, l_sc, acc_sc, *, tq, scale):
    i, j = pl.program_id(2), pl.program_id(3)

    @pl.when(j == 0)
    def _():
        m_sc[...] = jnp.full(m_sc.shape, NEG_BIG, F32)
        l_sc[...] = jnp.zeros(l_sc.shape, F32)
        acc_sc[...] = jnp.zeros(acc_sc.shape, F32)

    @pl.when(j <= i)
    def _():
        s = _dot_nt(q_ref[...], k_ref[...].astype(BF16)) * scale - f_ref[...]
        qpos = i * tq + lax.broadcasted_iota(jnp.int32, s.shape, 0)
        kpos = j * tq + lax.broadcasted_iota(jnp.int32, s.shape, 1)
        s = jnp.where(kpos <= qpos, s, NEG_BIG)
        m_new = jnp.maximum(m_sc[...], jnp.max(s, axis=-1, keepdims=True))
        alpha = jnp.exp(m_sc[...] - m_new)
        p = jnp.exp(s - m_new)
        l_sc[...] = alpha * l_sc[...] + jnp.sum(p, axis=-1, keepdims=True)
        acc_sc[...] = alpha * acc_sc[...] + _dot(p.astype(BF16), v_ref[...].astype(BF16))
        m_sc[...] = m_new

    @pl.when(j == pl.num_programs(3) - 1)
    def _():
        o_ref[...] = (acc_sc[...] / l_sc[...]).astype(o_ref.dtype)


def _attn_prompt(q, k, v, fcum, *, nb, seq, heads, hd, tq):
    t = q.shape[0]
    nq = seq // tq
    kv = lambda b, h, i, j: (b * nq + jnp.minimum(j, i), h)
    return pl.pallas_call(
        functools.partial(_attn_kernel, tq=tq, scale=hd ** -0.5),
        grid=(nb, heads, nq, nq),
        in_specs=[pl.BlockSpec((tq, hd), lambda b, h, i, j: (b * nq + i, h)),
                  pl.BlockSpec((tq, hd), kv),
                  pl.BlockSpec((tq, hd), kv),
                  pl.BlockSpec((None, 1, tq), lambda b, h, i, j: (h, 0, b * nq + jnp.minimum(j, i)))],
        out_specs=pl.BlockSpec((tq, hd), lambda b, h, i, j: (b * nq + i, h)),
        out_shape=jax.ShapeDtypeStruct((t, heads * hd), BF16),
        scratch_shapes=[pltpu.VMEM((tq, 1), F32), pltpu.VMEM((tq, 1), F32), pltpu.VMEM((tq, hd), F32)],
        compiler_params=_cparams(("arbitrary",) * 4, 32),
        name="attn_prompt",
    )(q, k, v, fcum)


def _attn_sample_kernel(pt_ref, q_ref, kn_ref, vn_ref, lfn_ref, *rest, pages, heads, hd, scale):
    k_refs, v_refs, lf_refs = rest[0:pages], rest[pages:2 * pages], rest[2 * pages:3 * pages]
    o_ref, qbd_sc, m_sc, l_sc, acc_sc, fc_sc = rest[3 * pages:]
    del pt_ref
    j = pl.program_id(1)
    nq = q_ref.shape[0]
    rows = heads * nq
    width = heads * hd
    psz = k_refs[0].shape[0]

    assert nq & (nq - 1) == 0

    def own_head():
        r = _div_pow2(lax.broadcasted_iota(jnp.int32, (rows, width), 0), nq)
        c = _div_pow2(lax.broadcasted_iota(jnp.int32, (rows, width), 1), hd)
        return r == c

    def head_rows(x):
        return jnp.concatenate([jnp.broadcast_to(x[h:h + 1], (nq, x.shape[1])) for h in range(heads)], axis=0)

    def update(s, vs):
        m_new = jnp.maximum(m_sc[...], jnp.max(s, axis=-1, keepdims=True))
        alpha = jnp.exp(m_sc[...] - m_new)
        p = jnp.exp(s - m_new)
        l_sc[...] = alpha * l_sc[...] + jnp.sum(p, axis=-1, keepdims=True)
        p = p.astype(BF16)
        n = vs[0].shape[0]
        pv = _dot(p[:, 0:n], vs[0])
        for c in range(1, len(vs)):
            pv = pv + _dot(p[:, c * n:(c + 1) * n], vs[c])
        acc_sc[...] = alpha * acc_sc[...] + pv
        m_sc[...] = m_new

    @pl.when(j == 0)
    def _():
        qt = jnp.concatenate([q_ref[...]] * heads, axis=0)
        qbd_sc[...] = jnp.where(own_head(), qt, 0.0).astype(BF16)
        m_sc[...] = jnp.full(m_sc.shape, NEG_BIG, F32)
        l_sc[...] = jnp.zeros(l_sc.shape, F32)
        acc_sc[...] = jnp.zeros(acc_sc.shape, F32)
        fc_sc[...] = jnp.zeros(fc_sc.shape, F32)

    qbd = qbd_sc[...]
    lf = jnp.concatenate([r_[...] for r_ in lf_refs], axis=1)
    fcum = fc_sc[...] + _lane_cumsum(lf)
    fc_sc[...] = fcum[:, pages * psz - 1:pages * psz]
    s = jnp.concatenate([_dot_nt(qbd, r_[...].astype(BF16)) for r_ in k_refs], axis=1)
    update(s * scale - head_rows(fcum), [r_[...].astype(BF16) for r_ in v_refs])

    @pl.when(j == pl.num_programs(1) - 1)
    def _():
        pad = jnp.zeros((LANES - nq, width), BF16)
        kn = jnp.concatenate([kn_ref[...].astype(BF16), pad], axis=0)
        vn = jnp.concatenate([vn_ref[...].astype(BF16), pad], axis=0)
        fnew = fc_sc[...] + _lane_cumsum(lfn_ref[...])
        sn = _dot_nt(qbd, kn) * scale - head_rows(fnew)
        tpos = lax.broadcasted_iota(jnp.int32, sn.shape, 0) & (nq - 1)
        spos = lax.broadcasted_iota(jnp.int32, sn.shape, 1)
        update(jnp.where(spos <= tpos, sn, NEG_BIG), [vn])
        o = jnp.where(own_head(), acc_sc[...] / l_sc[...], 0.0)
        out = o[0:nq]
        for h in range(1, heads):
            out = out + o[h * nq:(h + 1) * nq]
        o_ref[...] = out.astype(o_ref.dtype)


def _attn_sample(page_table, q, k_new, v_new, lf_new, cache_k, cache_v, cache_lft, layer, *, heads, hd, pages):
    nb, nq, width = q.shape
    n_pages = page_table.shape[1]
    psz = cache_k.shape[2]
    steps = n_pages // pages
    rows = heads * nq
    new = lambda w: pl.BlockSpec((None, nq, w), lambda b, j, pt: (b, 0, 0))

    def page_spec(shape, c):
        return pl.BlockSpec((None, None) + shape, lambda b, j, pt: (layer, pt[b, j * pages + c], 0, 0))

    in_specs = ([new(width), new(width), new(width),
                 pl.BlockSpec((None, heads, LANES), lambda b, j, pt: (b, 0, 0))]
                + [page_spec((psz, width), c) for c in range(pages)]
                + [page_spec((psz, width), c) for c in range(pages)]
                + [page_spec((heads, psz), c) for c in range(pages)])
    return pl.pallas_call(
        functools.partial(_attn_sample_kernel, pages=pages, heads=heads, hd=hd, scale=hd ** -0.5),
        grid_spec=pltpu.PrefetchScalarGridSpec(
            num_scalar_prefetch=1,
            grid=(nb, steps),
            in_specs=in_specs,
            out_specs=pl.BlockSpec((None, nq, width), lambda b, j, pt: (b, 0, 0)),
            scratch_shapes=[pltpu.VMEM((rows, width), BF16), pltpu.VMEM((rows, 1), F32),
                            pltpu.VMEM((rows, 1), F32), pltpu.VMEM((rows, width), F32),
                            pltpu.VMEM((heads, 1), F32)]),
        out_shape=jax.ShapeDtypeStruct((nb, nq, width), F32),
        compiler_params=_cparams(("arbitrary", "arbitrary"), 40),
        name="attn_sample",
    )(page_table, q, k_new, v_new, lf_new, *([cache_k] * pages), *([cache_v] * pages), *([cache_lft] * pages))


def _ssm_kernel(u_ref, bre_ref, bim_ref, cre_ref, cim_ref, d_ref, wglu_ref, abre_ref, abim_ref,
                zre_ref, zim_ref, h0re_ref, h0im_ref, o_ref, hre_ref, him_ref,
                sre, sim, cr, ci, *, nb, lane_chunk):
    c = pl.program_id(0)
    tr, n = sre.shape
    half = SUBLANES // 2

    @pl.when(c == 0)
    def _():
        cr[...] = h0re_ref[...]
        ci[...] = h0im_ref[...]

    u = u_ref[...]
    ub = u.astype(BF16)
    bu_re, bu_im = _dot(ub, bre_ref[...]), _dot(ub, bim_ref[...])
    z_re, z_im = zre_ref[...], zim_ref[...]
    sre[...] = z_re * bu_re - z_im * bu_im
    sim[...] = z_re * bu_im + z_im * bu_re

    for lc in range(n // lane_chunk):
        lanes = slice(lc * lane_chunk, (lc + 1) * lane_chunk)
        a_re = jnp.broadcast_to(abre_ref[:, lanes], (SUBLANES, lane_chunk))
        a_im = jnp.broadcast_to(abim_ref[:, lanes], (SUBLANES, lane_chunk))
        lower = lax.broadcasted_iota(jnp.int32, (SUBLANES, lane_chunk), 0) < half

        def step(hr, hi, xr, xi):
            return a_re * hr - a_im * hi + xr, a_re * hi + a_im * hr + xi

        def body(r, carry):
            hr, hi = carry
            row = pl.multiple_of(r * SUBLANES, SUBLANES)
            xr, xi = sre[pl.ds(row, SUBLANES), lanes], sim[pl.ds(row, SUBLANES), lanes]
            if nb == SUBLANES:
                nr, ni = step(hr, hi, xr, xi)
            else:
                t1r, t1i = step(pltpu.roll(hr, half, axis=0), pltpu.roll(hi, half, axis=0), xr, xi)
                t2r, t2i = step(pltpu.roll(t1r, half, axis=0), pltpu.roll(t1i, half, axis=0), xr, xi)
                nr, ni = jnp.where(lower, t1r, t2r), jnp.where(lower, t1i, t2i)
            sre[pl.ds(row, SUBLANES), lanes] = nr
            sim[pl.ds(row, SUBLANES), lanes] = ni
            return nr, ni

        hr, hi = lax.fori_loop(0, tr // SUBLANES, body, (cr[:, lanes], ci[:, lanes]))
        cr[:, lanes] = hr
        ci[:, lanes] = hi

    y = _dot(sre[...].astype(BF16), cre_ref[...]) - _dot(sim[...].astype(BF16), cim_ref[...]) + d_ref[...] * u
    y = _gelu_tanh(y)
    o_ref[...] = (y * jax.nn.sigmoid(_dot(y.astype(BF16), wglu_ref[...]))).astype(o_ref.dtype)

    @pl.when(c == pl.num_programs(0) - 1)
    def _():
        hre_ref[...] = cr[...]
        him_ref[...] = ci[...]


def _ssm(u, bre, bim, cre, cim, dvec, wglu, ab_re, ab_im, z_re, z_im, h0_re, h0_im, *, nb, tr):
    rows, w = u.shape
    n = bre.shape[1]
    assert nb in (SUBLANES // 2, SUBLANES)
    full = lambda a: _resident(a.shape)
    lead = jnp.zeros((SUBLANES - nb, n), F32)
    h0_re, h0_im = jnp.concatenate([lead, h0_re], axis=0), jnp.concatenate([lead, h0_im], axis=0)
    o, h_re, h_im = pl.pallas_call(
        functools.partial(_ssm_kernel, nb=nb, lane_chunk=512),
        grid=(rows // tr,),
        in_specs=[pl.BlockSpec((tr, w), lambda c: (c, 0)),
                  full(bre), full(bim), full(cre), full(cim), full(dvec), full(wglu),
                  full(ab_re), full(ab_im), full(z_re), full(z_im), full(h0_re), full(h0_im)],
        out_specs=[pl.BlockSpec((tr, w), lambda c: (c, 0)),
                   pl.BlockSpec((SUBLANES, n), lambda c: (0, 0)), pl.BlockSpec((SUBLANES, n), lambda c: (0, 0))],
        out_shape=[jax.ShapeDtypeStruct((rows, w), BF16),
                   jax.ShapeDtypeStruct((SUBLANES, n), F32), jax.ShapeDtypeStruct((SUBLANES, n), F32)],
        scratch_shapes=[pltpu.VMEM((tr, n), F32), pltpu.VMEM((tr, n), F32),
                        pltpu.VMEM((SUBLANES, n), F32), pltpu.VMEM((SUBLANES, n), F32)],
        compiler_params=_cparams(("arbitrary",), 48),
        name="ssm",
    )(u, bre, bim, cre, cim, dvec, wglu, ab_re, ab_im, z_re, z_im, h0_re, h0_im)
    return o, h_re[SUBLANES - nb:], h_im[SUBLANES - nb:]


def _pool_kernel(u_ref, buf_ref, w_ref, sc_ref, o_ref, nb_ref, ext, *, rs, pos0):
    n = u_ref.shape[0]
    hdr = POOL_HDR * rs
    gw = w_ref.shape[1]
    ext[0:rs, :] = jnp.zeros((rs, ext.shape[1]), F32)
    ext[rs:hdr, :] = buf_ref[...]
    ext[hdr:hdr + n, :] = u_ref[...]
    pos = pos0 + _div_pow2(lax.broadcasted_iota(jnp.int32, (n, gw), 0), rs)
    for gi, win in enumerate(POOL_WINDOWS):
        lanes = slice(gi * gw, (gi + 1) * gw)
        tot = ext[hdr:hdr + n, lanes]
        for back in range(1, win):
            tot = tot + ext[hdr - back * rs:hdr - back * rs + n, lanes]
        cnt = jnp.minimum(pos + 1, win).astype(F32)
        pooled = tot / cnt - u_ref[:, lanes]
        o_ref[:, lanes] = (_dot(pooled.astype(BF16), w_ref[gi]) * sc_ref[:, lanes]).astype(o_ref.dtype)
    nb_ref[...] = ext[n + rs:n + hdr, :]


def _pool(u, buf, w, scale, *, blocks, rs, pos0):
    t, width = u.shape
    n = t // blocks
    hist = POOL_BUF * rs
    return pl.pallas_call(
        functools.partial(_pool_kernel, rs=rs, pos0=pos0),
        grid=(blocks,),
        in_specs=[pl.BlockSpec((n, width), lambda b: (b, 0)),
                  pl.BlockSpec((None, hist, width), lambda b: (b, 0, 0)),
                  _resident(w.shape), _resident(scale.shape)],
        out_specs=[pl.BlockSpec((n, width), lambda b: (b, 0)),
                   pl.BlockSpec((None, hist, width), lambda b: (b, 0, 0))],
        out_shape=[jax.ShapeDtypeStruct((t, width), BF16), jax.ShapeDtypeStruct((blocks, hist, width), F32)],
        scratch_shapes=[pltpu.VMEM((POOL_HDR * rs + n, width), F32)],
        compiler_params=_cparams(("arbitrary",), 48),
        name="pool",
    )(u, buf, w, scale)


def _merge_kernel(gt_ref, oa_ref, os_ref, op_ref, x_ref, gm_ref, gpost_ref, gpre_ref, scf_ref, shf_ref,
                  wb_ref, wo_ref, xo_ref, h2_ref, *, d, aw, sw):
    merged = (gt_ref[:, 0:d] * _dot(oa_ref[...], wb_ref[0:aw, :])
              + gt_ref[:, d:2 * d] * _dot(os_ref[...], wb_ref[aw:aw + sw, :])
              + gt_ref[:, 2 * d:3 * d] * _dot(op_ref[...], wb_ref[aw + sw:, :]))
    y = _dot(merged.astype(BF16), wo_ref[...])
    x = x_ref[...] + gm_ref[...] * _rms(y, gpost_ref[...])
    xo_ref[...] = x
    h2_ref[...] = (_rms(x, gpre_ref[...]) * (1.0 + scf_ref[...]) + shf_ref[...]).astype(h2_ref.dtype)


def _merge(gates, o_att, o_ssm, o_pool, x, mod, g_post, g_pre_ffn, wb, wo, *, tm, seq, time_major, aw, sw):
    t, d = x.shape
    rb = mod.shape[2]
    if time_major:
        bidx = lambda i: 0
        os_spec = pl.BlockSpec((tm, sw), lambda i: (i, 0))
    else:
        tpb = seq // tm
        bidx = lambda i: i // tpb
        os_spec = pl.BlockSpec((tm, sw), lambda i: (i % tpb, i // tpb))
    row = lambda w: pl.BlockSpec((tm, w), lambda i: (i, 0))
    modspec = lambda kk: pl.BlockSpec((None, None, rb, d), lambda i: (kk, bidx(i), 0, 0))
    return pl.pallas_call(
        functools.partial(_merge_kernel, d=d, aw=aw, sw=sw),
        grid=(t // tm,),
        in_specs=[row(3 * d), row(aw), os_spec, row(o_pool.shape[1]), row(d),
                  modspec(2), _resident((1, d)), _resident((1, d)), modspec(4), modspec(3),
                  _resident(wb.shape), _resident(wo.shape)],
        out_specs=[row(d), row(d)],
        out_shape=[jax.ShapeDtypeStruct((t, d), F32), jax.ShapeDtypeStruct((t, d), BF16)],
        compiler_params=_cparams(("arbitrary",), 56),
        name="merge",
    )(gates, o_att, o_ssm, o_pool, x, mod, g_post, g_pre_ffn, mod, mod, wb, wo)


def _ffn_up_kernel(h_ref, halo_ref, wg_ref, wv_ref, cwg_ref, cwv_ref, cbg_ref, cbv_ref, bg_ref, bv_ref,
                   act_ref, ncg_ref, ncv_ref, extg, extv, *, rs, tpb):
    i = pl.program_id(1)
    tm = h_ref.shape[0]
    hb = h_ref[...]
    back = CONV_BUF * rs

    def half(w_ref, cw_ref, cb_ref, buf_ref, ext, nc_ref):
        ext[CONV_HDR:CONV_HDR + tm, :] = _dot(hb, w_ref[...])
        if rs == 1:
            ext[CONV_HDR - SUBLANES:CONV_HDR, :] = _dot(halo_ref[...], w_ref[...])

            @pl.when(i % tpb == 0)
            def _():
                ext[CONV_HDR - back:CONV_HDR, :] = buf_ref[...]
        else:
            ext[CONV_HDR - back:CONV_HDR, :] = buf_ref[...]
        conv = cb_ref[...] + cw_ref[CONV_WIDTH - 1:CONV_WIDTH, :] * ext[CONV_HDR:CONV_HDR + tm, :]
        for tap in range(CONV_WIDTH - 1):
            off = CONV_HDR - (CONV_WIDTH - 1 - tap) * rs
            conv = conv + cw_ref[tap:tap + 1, :] * ext[off:off + tm, :]
        nc_ref[...] = ext[CONV_HDR + tm - back:CONV_HDR + tm, :]
        return conv

    gate = half(wg_ref, cwg_ref, cbg_ref, bg_ref, extg, ncg_ref)
    val = half(wv_ref, cwv_ref, cbv_ref, bv_ref, extv, ncv_ref)
    act_ref[...] = (_gelu_tanh(gate) * val).astype(act_ref.dtype)


def _ffn_up(h2, w_up, conv_w, conv_b, buf, *, tm, seq, rs, tn):
    t, d = h2.shape
    f = w_up.shape[1] // 2
    nj = f // tn
    back = CONV_BUF * rs
    nblk = buf.shape[0]
    tpb = max(seq // tm, 1) if rs == 1 else 1
    blk = (lambda i: i // tpb) if rs == 1 else (lambda i: 0)
    halo_rows = tm // SUBLANES
    col = lambda rows, off: pl.BlockSpec((rows, tn), lambda j, i: (0, j + off))
    state = lambda off: pl.BlockSpec((None, back, tn), lambda j, i: (blk(i), 0, j + off))
    return pl.pallas_call(
        functools.partial(_ffn_up_kernel, rs=rs, tpb=tpb),
        grid=(nj, t // tm),
        in_specs=[pl.BlockSpec((tm, d), lambda j, i: (i, 0)),
                  pl.BlockSpec((SUBLANES, d), lambda j, i: (jnp.maximum(i * halo_rows - 1, 0), 0)),
                  col(d, 0), col(d, nj), col(CONV_WIDTH, 0), col(CONV_WIDTH, nj), col(1, 0), col(1, nj),
                  state(0), state(nj)],
        out_specs=[pl.BlockSpec((tm, tn), lambda j, i: (i, j)),
                   pl.BlockSpec((None, back, tn), lambda j, i: (blk(i), 0, j)),
                   pl.BlockSpec((None, back, tn), lambda j, i: (blk(i), 0, j))],
        out_shape=[jax.ShapeDtypeStruct((t, f), BF16),
                   jax.ShapeDtypeStruct((nblk, back, f), F32), jax.ShapeDtypeStruct((nblk, back, f), F32)],
        scratch_shapes=[pltpu.VMEM((CONV_HDR + tm, tn), F32), pltpu.VMEM((CONV_HDR + tm, tn), F32)],
        compiler_params=_cparams(("arbitrary", "arbitrary"), 48),
        name="ffn_up",
    )(h2, h2, w_up, w_up, conv_w, conv_w, conv_b, conv_b, buf, buf)


def _ffn_down_kernel(act_ref, w_ref, x_ref, gf_ref, gpost_ref, o_ref):
    o_ref[...] = x_ref[...] + gf_ref[...] * _rms(_dot(act_ref[...], w_ref[...]), gpost_ref[...])


def _ffn_down(act, w_down, x, mod, g_post, *, tm, seq, time_major):
    t, d = x.shape
    f = act.shape[1]
    rb = mod.shape[2]
    tpb = 1 if time_major else seq // tm
    bidx = (lambda i: 0) if time_major else (lambda i: i // tpb)
    return pl.pallas_call(
        _ffn_down_kernel,
        grid=(t // tm,),
        in_specs=[pl.BlockSpec((tm, f), lambda i: (i, 0)), _resident(w_down.shape),
                  pl.BlockSpec((tm, d), lambda i: (i, 0)),
                  pl.BlockSpec((None, None, rb, d), lambda i: (5, bidx(i), 0, 0)), _resident((1, d))],
        out_specs=pl.BlockSpec((tm, d), lambda i: (i, 0)),
        out_shape=jax.ShapeDtypeStruct((t, d), F32),
        compiler_params=_cparams(("arbitrary",), 56),
        name="ffn_down",
    )(act, w_down, x, mod, g_post)


def _block_diag(w):
    g, r, c = w.shape
    eye = jnp.eye(g, dtype=w.dtype)
    return (w[:, :, None, :] * eye[:, None, :, None]).reshape(g * r, g * c)


def _pick(n, pref):
    return pref if n % pref == 0 else n


def _layer(x, mod, lw, attn_fn, h0_re, h0_im, pool_buf, conv_buf, *, nb, seq, time_major, pos0, dims):
    aw, sw, pw = dims["aw"], dims["sw"], dims["pw"]
    t, d = x.shape
    rs = nb if time_major else 1
    tm_in = t if time_major else _pick(seq, 256)
    h, q, k, v, u_ssm, u_pool, lft = _inproj(
        x, lw["g_pre_mix"], mod, lw["wa"], lw["wft"], lw["b_f"],
        tm=tm_in, seq=seq, nb=nb, aw=aw, sw=sw, pw=pw, time_major=time_major)
    gates = _gates(h, lw["wg"], tm=t if time_major else _pick(t, 1024))
    o_att = attn_fn(q, k, v, lft)
    o_ssm, h_re, h_im = _ssm(
        u_ssm.reshape(t, sw), lw["bre"], lw["bim"], lw["cre"], lw["cim"], lw["ssm_d"], lw["w_glu"],
        lw["ab_re"], lw["ab_im"], lw["z_re"], lw["z_im"], h0_re, h0_im, nb=nb, tr=_pick(t, 512))
    if not time_major:
        o_ssm = o_ssm.reshape(seq, nb * sw)
    o_pool, new_pool = _pool(u_pool, pool_buf, lw["pool_w"], lw["pool_scale"],
                             blocks=1 if time_major else nb, rs=rs, pos0=pos0)
    x, h2 = _merge(gates, o_att, o_ssm, o_pool, x, mod, lw["g_post_mix"], lw["g_pre_ffn"], lw["w_branch"],
                   lw["w_out"], tm=tm_in, seq=seq, time_major=time_major, aw=aw, sw=sw)
    act, ncg, ncv = _ffn_up(h2, lw["w_up"], lw["conv_w"], lw["conv_b"], conv_buf,
                            tm=t if time_major else _pick(seq, 1024), seq=seq, rs=rs, tn=dims["tn_ff"])
    x = _ffn_down(act, lw["w_down"], x, mod, lw["g_post_ffn"], tm=tm_in, seq=seq, time_major=time_major)
    return x, k, v, lft, h_re, h_im, new_pool, jnp.concatenate([ncg, ncv], axis=-1)


def kernel(x_prompt, x_sample, cache_k, cache_v, cache_logf, page_table, state_ssm_re, state_ssm_im, state_pool, state_ffn_conv, c_prompt, c_sample, w_ada, b_ada, g_pre_mix, g_post_mix, g_pre_ffn, g_post_ffn, w_in, b_f, ssm_a_re, ssm_a_im, ssm_log_step, ssm_b_re, ssm_b_im, ssm_c_re, ssm_c_im, ssm_d, w_glu, pool_w, pool_scale, w_branch, w_out, w_up, conv_w, conv_b, w_down):
    bp, seq, d = x_prompt.shape
    bs, ds, _ = x_sample.shape
    depth, n_pool, psz, heads, hd = cache_k.shape
    aw = heads * hd
    groups, nstate = ssm_a_re.shape[1], ssm_a_re.shape[2]
    sw = ssm_d.shape[1]
    pw = pool_scale.shape[1]
    f = w_down.shape[1]
    past = page_table.shape[1] * psz
    tp, ts = bp * seq, bs * ds
    off_f = 3 * aw
    off_ssm = off_f + heads
    off_gate = off_ssm + sw + pw
    dims = dict(aw=aw, sw=sw, pw=pw, tn_ff=_pick(f, 512))

    wa = jnp.concatenate([w_in[:, :, :off_f], w_in[:, :, off_ssm:off_gate]], axis=-1).astype(BF16)
    wft = jnp.swapaxes(w_in[:, :, off_f:off_ssm], 1, 2).astype(BF16)
    wg = w_in[:, :, off_gate:].astype(BF16)
    ab_re, ab_im, z_re, z_im = _ssm_disc(ssm_a_re, ssm_a_im, ssm_log_step)
    layers = []
    for l in range(depth):
        layers.append(dict(
            g_pre_mix=g_pre_mix[l][None], g_post_mix=g_post_mix[l][None],
            g_pre_ffn=g_pre_ffn[l][None], g_post_ffn=g_post_ffn[l][None],
            wa=wa[l], wft=wft[l], wg=wg[l], b_f=b_f[l][:, None],
            bre=_block_diag(jnp.swapaxes(ssm_b_re[l], 1, 2)).astype(BF16),
            bim=_block_diag(jnp.swapaxes(ssm_b_im[l], 1, 2)).astype(BF16),
            cre=_block_diag(jnp.swapaxes(ssm_c_re[l], 1, 2)).astype(BF16),
            cim=_block_diag(jnp.swapaxes(ssm_c_im[l], 1, 2)).astype(BF16),
            ssm_d=ssm_d[l][None], w_glu=w_glu[l].astype(BF16),
            ab_re=ab_re[l][None], ab_im=ab_im[l][None], z_re=z_re[l][None], z_im=z_im[l][None],
            pool_w=pool_w[l].astype(BF16), pool_scale=pool_scale[l][None],
            w_branch=w_branch[l].astype(BF16), w_out=w_out[l].astype(BF16),
            w_up=w_up[l].astype(BF16), conv_w=conv_w[l], conv_b=conv_b[l][None],
            w_down=w_down[l].astype(BF16)))

    n_c = bp + bs
    c_rows = jnp.concatenate([c_prompt, c_sample, jnp.zeros((-n_c % 16, d), F32)], axis=0).astype(BF16)
    mod_all = _ada(c_rows, w_ada, b_ada)

    cache_k2 = cache_k.reshape(depth, n_pool, psz, aw)
    cache_v2 = cache_v.reshape(depth, n_pool, psz, aw)
    cache_lft = jnp.swapaxes(cache_logf, 2, 3)
    pages = 4 if page_table.shape[1] % 4 == 0 else 1

    def to_tm(a):
        return jnp.swapaxes(a.reshape((bs, ds) + a.shape[1:]), 0, 1).reshape((ts,) + a.shape[1:])

    def from_tm(a):
        return jnp.swapaxes(a.reshape((ds, bs) + a.shape[1:]), 0, 1)

    xp = x_prompt.reshape(tp, d)
    xs = to_tm(x_sample.reshape(ts, d))
    outs_p, outs_s = [], []
    for l in range(depth):
        lw = layers[l]
        mod_p = jnp.swapaxes(mod_all[l, :bp].reshape(bp, 6, 1, d), 0, 1)

        def attn_p(q, k, v, lft):
            fcum = _fcum(lft, seq).reshape(heads, 1, tp)
            return _attn_prompt(q, k, v, fcum, nb=bp, seq=seq, heads=heads, hd=hd, tq=_pick(seq, 512))

        zst = jnp.zeros((bp, groups * nstate), F32)
        xp, k, v, lft, h_re, h_im, new_pool, new_conv = _layer(
            xp, mod_p, lw, attn_p, zst, zst, jnp.zeros((bp, POOL_BUF, pw), F32),
            jnp.zeros((bp, CONV_BUF, 2 * f), F32),
            nb=bp, seq=seq, time_major=False, pos0=0, dims=dims)
        outs_p.append((k.reshape(bp, seq, heads, hd), v.reshape(bp, seq, heads, hd),
                       lft.T.reshape(bp, seq, heads),
                       h_re.reshape(bp, groups, nstate), h_im.reshape(bp, groups, nstate),
                       new_pool, new_conv))

        mod_s = jnp.tile(mod_all[l, bp:n_c], (ds, 1)).reshape(ts, 6, d).transpose(1, 0, 2)[:, None]

        def attn_s(q, k, v, lft, l=l):
            lf_new = jnp.transpose(lft.reshape(heads, ds, bs), (2, 0, 1))
            lf_new = jnp.pad(lf_new, ((0, 0), (0, 0), (0, LANES - ds)))
            o = _attn_sample(page_table, from_tm(q.astype(F32)), from_tm(k), from_tm(v), lf_new,
                             cache_k2, cache_v2, cache_lft, l, heads=heads, hd=hd, pages=pages)
            return to_tm(o.reshape(ts, aw)).astype(BF16)

        pool_buf_s = jnp.swapaxes(state_pool[l], 0, 1).reshape(1, POOL_BUF * bs, pw)
        conv_buf_s = jnp.swapaxes(state_ffn_conv[l], 0, 1).reshape(1, CONV_BUF * bs, 2 * f)
        xs, k, v, lft, h_re, h_im, new_pool, new_conv = _layer(
            xs, mod_s, lw, attn_s,
            state_ssm_re[l].reshape(bs, groups * nstate), state_ssm_im[l].reshape(bs, groups * nstate),
            pool_buf_s, conv_buf_s, nb=bs, seq=ds, time_major=True, pos0=past, dims=dims)
        outs_s.append((from_tm(k).reshape(bs, ds, heads, hd), from_tm(v).reshape(bs, ds, heads, hd),
                       jnp.transpose(lft.reshape(heads, ds, bs), (2, 1, 0)),
                       h_re.reshape(bs, groups, nstate), h_im.reshape(bs, groups, nstate),
                       jnp.swapaxes(new_pool.reshape(POOL_BUF, bs, pw), 0, 1),
                       jnp.swapaxes(new_conv.reshape(CONV_BUF, bs, 2 * f), 0, 1)))

    stack = lambda outs: tuple(jnp.stack([o[i] for o in outs]) for i in range(7))
    return ((xp.reshape(bp, seq, d), from_tm(xs)) + stack(outs_p) + stack(outs_s))
```

```python
import functools
import math

import jax
import jax.numpy as jnp
from jax import lax
from jax.experimental import pallas as pl
from jax.experimental.pallas import tpu as pltpu

F32 = jnp.float32
BF16 = jnp.bfloat16

EPS = 1e-6
POOL_WINDOWS = (2, 4, 8, 16)
POOL_BUF = max(POOL_WINDOWS) - 1
POOL_HDR = POOL_BUF + 1
CONV_WIDTH = 3
CONV_BUF = CONV_WIDTH - 1
CONV_HDR = 16
SUBLANES = 8
LANES = 128
MIB = 1024 * 1024
NEG_BIG = -1e30
LOG2E = math.log2(math.e)


def _cparams(semantics, vmem_mib):
    return pltpu.CompilerParams(dimension_semantics=semantics, vmem_limit_bytes=vmem_mib * MIB)


def _resident(shape):
    nd = len(shape)
    return pl.BlockSpec(shape, lambda *_: (0,) * nd, pipeline_mode=pl.Buffered(1))


def _dot(a, b):
    return jnp.dot(a, b, preferred_element_type=F32)


def _dot_nt(a, b):
    return lax.dot_general(a, b, (((1,), (1,)), ((), ())), preferred_element_type=F32)


def _rms(x, g):
    return x * lax.rsqrt(jnp.mean(x * x, axis=-1, keepdims=True) + EPS) * g


def _gelu_tanh(x):
    c = math.sqrt(2.0 / math.pi)
    return 0.5 * x * (1.0 + jnp.tanh(c * (x + 0.044715 * (x * x * x))))


def _log_sigmoid(x):
    return jnp.minimum(x, 0.0) - jnp.log1p(jnp.exp(-jnp.abs(x)))


def _div_pow2(x, c):
    assert c > 0 and c & (c - 1) == 0
    return x >> (c.bit_length() - 1)


def _row_cumsum(x):
    n = x.shape[0]
    row = lax.broadcasted_iota(jnp.int32, x.shape, 0)
    k = 1
    while k < n:
        x = x + jnp.where(row >= k, pltpu.roll(x, k, axis=0), 0.0)
        k *= 2
    return x


def _ada_kernel(c_ref, w_ref, b_ref, o_ref):
    o_ref[...] = _dot(c_ref[...], w_ref[...].astype(BF16)) + b_ref[...]


def _ada(c_rows, w_ada, b_ada):
    depth, d, n = w_ada.shape
    r = c_rows.shape[0]
    tn = 1024
    return pl.pallas_call(
        _ada_kernel,
        grid=(depth, n // tn),
        in_specs=[pl.BlockSpec((r, d), lambda l, j: (0, 0)),
                  pl.BlockSpec((None, d, tn), lambda l, j: (l, 0, j)),
                  pl.BlockSpec((None, 1, tn), lambda l, j: (l, 0, j))],
        out_specs=pl.BlockSpec((None, r, tn), lambda l, j: (l, 0, j)),
        out_shape=jax.ShapeDtypeStruct((depth, r, n), F32),
        compiler_params=_cparams(("arbitrary", "arbitrary"), 40),
        name="ada",
    )(c_rows, w_ada, b_ada.reshape(depth, 1, n))


def _ssm_disc_kernel(are_ref, aim_ref, ls_ref, abre_ref, abim_ref, zre_ref, zim_ref):
    a_re, a_im = are_ref[...], aim_ref[...]
    step = jnp.exp(ls_ref[...])
    mag = jnp.exp(step * a_re)
    ab_re = mag * jnp.cos(step * a_im)
    ab_im = mag * jnp.sin(step * a_im)
    den = a_re * a_re + a_im * a_im
    abre_ref[...] = ab_re
    abim_ref[...] = ab_im
    zre_ref[...] = ((ab_re - 1.0) * a_re + ab_im * a_im) / den
    zim_ref[...] = (ab_im * a_re - (ab_re - 1.0) * a_im) / den


def _ssm_disc(a_re, a_im, log_step):
    depth, g, p = a_re.shape
    n = g * p
    ls = jnp.broadcast_to(log_step[:, :, None], (depth, g, p)).reshape(depth, n)
    spec = pl.BlockSpec((depth, n), lambda: (0, 0))
    return pl.pallas_call(
        _ssm_disc_kernel,
        in_specs=[spec] * 3,
        out_specs=[spec] * 4,
        out_shape=[jax.ShapeDtypeStruct((depth, n), F32)] * 4,
        name="ssm_disc",
    )(a_re.reshape(depth, n), a_im.reshape(depth, n), ls)


def _inproj_kernel(x_ref, g_ref, sc_ref, sh_ref, wqt_ref, wvt_ref, wk_ref, wf_ref, bf_ref,
                   h_ref, q_ref, k_ref, v_ref, us_ref, up_ref, lf_ref, *, aw, sw, pw, transposed, c2):
    h = _rms(x_ref[...], g_ref[...]) * (1.0 + sc_ref[...]) + sh_ref[...]
    hb = h.astype(BF16)
    h_ref[...] = hb
    if transposed:
        q_ref[...] = (_dot_nt(wqt_ref[...], hb) * c2).astype(q_ref.dtype)
        v_ref[...] = _dot_nt(wvt_ref[...], hb)
    else:
        q_ref[...] = _dot_nt(hb, wqt_ref[...])
        v_ref[...] = _dot_nt(hb, wvt_ref[...])
    k_ref[...] = _dot(hb, wk_ref[:, 0:aw])
    us_ref[...] = _dot(hb, wk_ref[:, aw:aw + sw])
    up_ref[...] = _dot(hb, wk_ref[:, aw + sw:aw + sw + pw])
    lf_ref[...] = _log_sigmoid(_dot(hb, wf_ref[...]) + bf_ref[...])


def _inproj(x, g, mod, wqt, wvt, wk, wf, bf, *, tm, seq, nb, sw, pw, time_major, c2):
    t, d = x.shape
    aw = wqt.shape[0]
    rb = mod.shape[2]
    row = lambda w: pl.BlockSpec((tm, w), lambda i: (i, 0))
    colT = pl.BlockSpec((aw, tm), lambda i: (0, i))
    if time_major:
        assert tm == t
        bidx = lambda i: 0
        us_shape, us_spec = (t, sw), pl.BlockSpec((tm, sw), lambda i: (i, 0))
        q_shape, q_spec, v_shape, v_spec = (t, aw), row(aw), (t, aw), row(aw)
        q_dtype = F32
    else:
        tpb = seq // tm
        bidx = lambda i: i // tpb
        us_shape, us_spec = (seq, nb * sw), pl.BlockSpec((tm, sw), lambda i: (i % tpb, i // tpb))
        q_shape, q_spec, v_shape, v_spec = (aw, t), colT, (aw, t), colT
        q_dtype = BF16
    modspec = lambda kk: pl.BlockSpec((None, None, rb, d), lambda i: (kk, bidx(i), 0, 0))
    return pl.pallas_call(
        functools.partial(_inproj_kernel, aw=aw, sw=sw, pw=pw, transposed=not time_major, c2=c2),
        grid=(t // tm,),
        in_specs=[row(d), _resident((1, d)), modspec(1), modspec(0),
                  _resident(wqt.shape), _resident(wvt.shape), _resident(wk.shape),
                  _resident(wf.shape), _resident(bf.shape)],
        out_specs=[row(d), q_spec, row(aw), v_spec, us_spec, row(pw), row(LANES)],
        out_shape=[jax.ShapeDtypeStruct((t, d), BF16), jax.ShapeDtypeStruct(q_shape, q_dtype),
                   jax.ShapeDtypeStruct((t, aw), F32), jax.ShapeDtypeStruct(v_shape, F32),
                   jax.ShapeDtypeStruct(us_shape, F32), jax.ShapeDtypeStruct((t, pw), F32),
                   jax.ShapeDtypeStruct((t, LANES), F32)],
        compiler_params=_cparams(("arbitrary",), 48),
        name="inproj",
    )(x, g, mod, mod, wqt, wvt, wk, wf, bf)


def _gates_kernel(h_ref, w_ref, o_ref):
    o_ref[...] = jax.nn.sigmoid(_dot(h_ref[...], w_ref[...]))


def _gates(h, wg, *, tm):
    t, d = h.shape
    n = wg.shape[1]
    tn = 1024
    return pl.pallas_call(
        _gates_kernel,
        grid=(n // tn, t // tm),
        in_specs=[pl.BlockSpec((tm, d), lambda j, i: (i, 0)),
                  pl.BlockSpec((d, tn), lambda j, i: (0, j))],
        out_specs=pl.BlockSpec((tm, tn), lambda j, i: (i, j)),
        out_shape=jax.ShapeDtypeStruct((t, n), F32),
        compiler_params=_cparams(("arbitrary", "arbitrary"), 48),
        name="gates",
    )(h, wg)


def _fcum_kernel(lf_ref, o_ref, *, heads):
    neg = _row_cumsum(lf_ref[...]) * (-LOG2E)
    hi = neg.astype(BF16).astype(F32)
    r1 = neg - hi
    mid = r1.astype(BF16).astype(F32)
    lo = (r1 - mid).astype(BF16).astype(F32)
    lane = lax.broadcasted_iota(jnp.int32, neg.shape, 1)
    out = jnp.where(lane < heads, hi,
                    jnp.where(lane < 2 * heads, pltpu.roll(mid, heads, axis=1),
                              jnp.where(lane < 3 * heads, pltpu.roll(lo, 2 * heads, axis=1), 0.0)))
    o_ref[...] = out.astype(o_ref.dtype)


def _fcum(lf, seq, heads):
    t, w = lf.shape
    return pl.pallas_call(
        functools.partial(_fcum_kernel, heads=heads),
        grid=(t // seq,),
        in_specs=[pl.BlockSpec((seq, w), lambda b: (b, 0))],
        out_specs=pl.BlockSpec((seq, w), lambda b: (b, 0)),
        out_shape=jax.ShapeDtypeStruct((t, w), BF16),
        name="fcum",
    )(lf)


def _attn_kernel(qt_ref, k_ref, vt_ref, f_ref, o_ref, m_sc, l_sc, acc_sc, *, hd, hpb, heads):
    i, j = pl.program_id(2), pl.program_id(3)
    tq = qt_ref.shape[1]

    @pl.when(j == 0)
    def _():
        m_sc[...] = jnp.full(m_sc.shape, NEG_BIG, F32)
        l_sc[...] = jnp.zeros(l_sc.shape, F32)
        acc_sc[...] = jnp.zeros(acc_sc.shape, F32)

    def block(diagonal):
        for g in range(hpb):
            rows = slice(g * hd, (g + 1) * hd)
            head = pl.program_id(1) * hpb + g
            r = lax.broadcasted_iota(jnp.int32, (LANES, tq), 0)
            pick = jnp.where(r < 3 * heads, jnp.where((r & (heads - 1)) == head, 1.0, 0.0), 0.0).astype(BF16)
            qaug = jnp.concatenate([qt_ref[rows, :], pick], axis=0)
            kaug = jnp.concatenate([k_ref[:, rows].astype(BF16), f_ref[...]], axis=1)
            t = _dot(kaug, qaug)
            if diagonal:
                kpos = lax.broadcasted_iota(jnp.int32, t.shape, 0)
                qpos = lax.broadcasted_iota(jnp.int32, t.shape, 1)
                t = jnp.where(kpos <= qpos, t, NEG_BIG)
            m_old = m_sc[g]
            m_new = jnp.maximum(m_old, jnp.max(t, axis=0, keepdims=True))
            alpha = jnp.exp2(m_old - m_new)
            p = jnp.exp2(t - m_new)
            l_sc[g] = alpha * l_sc[g] + jnp.sum(p, axis=0, keepdims=True)
            acc_sc[rows, :] = alpha * acc_sc[rows, :] + _dot(vt_ref[rows, :].astype(BF16), p.astype(BF16))
            m_sc[g] = m_new

    @pl.when(j < i)
    def _():
        block(False)

    @pl.when(j == i)
    def _():
        block(True)
        for g in range(hpb):
            rows = slice(g * hd, (g + 1) * hd)
            o_ref[:, rows] = (acc_sc[rows, :] / l_sc[g]).T.astype(o_ref.dtype)


def _attn_prompt(qt, k, vt, faug, *, nb, seq, heads, hd, tq, hpb):
    t = k.shape[0]
    nq = seq // tq
    w = hpb * hd
    assert heads & (heads - 1) == 0 and 3 * heads <= LANES
    kblk = lambda b, i, j: b * nq + jnp.minimum(j, i)
    return pl.pallas_call(
        functools.partial(_attn_kernel, hd=hd, hpb=hpb, heads=heads),
        grid=(nb, heads // hpb, nq, nq),
        in_specs=[pl.BlockSpec((w, tq), lambda b, h, i, j: (h, b * nq + i)),
                  pl.BlockSpec((tq, w), lambda b, h, i, j: (kblk(b, i, j), h)),
                  pl.BlockSpec((w, tq), lambda b, h, i, j: (h, kblk(b, i, j))),
                  pl.BlockSpec((tq, LANES), lambda b, h, i, j: (kblk(b, i, j), 0))],
        out_specs=pl.BlockSpec((tq, w), lambda b, h, i, j: (b * nq + i, h)),
        out_shape=jax.ShapeDtypeStruct((t, heads * hd), BF16),
        scratch_shapes=[pltpu.VMEM((hpb, 1, tq), F32), pltpu.VMEM((hpb, 1, tq), F32),
                        pltpu.VMEM((w, tq), F32)],
        compiler_params=_cparams(("arbitrary",) * 4, 40),
        name="attn_prompt",
    )(qt, k, vt, faug)


def _attn_sample_kernel(pt_ref, q_ref, kn_ref, vn_ref, lfn_ref, *rest, pages, heads, hd, c2):
    k_refs, v_refs, lf_refs = rest[0:pages], rest[pages:2 * pages], rest[2 * pages:3 * pages]
    o_ref, q2_sc, m_sc, l_sc, acc_sc, fc_sc = rest[3 * pages:]
    del pt_ref
    j = pl.program_id(1)
    nq = q_ref.shape[0]
    rows = heads * nq
    psz = k_refs[0].shape[0]
    cols = psz * heads
    assert nq & (nq - 1) == 0 and heads & (heads - 1) == 0

    @pl.when(j == 0)
    def _():
        q2 = jnp.concatenate([q_ref[:, h * hd:(h + 1) * hd] for h in range(heads)], axis=0)
        q2_sc[...] = (q2 * c2).astype(BF16)
        m_sc[...] = jnp.full(m_sc.shape, NEG_BIG, F32)
        l_sc[...] = jnp.zeros(l_sc.shape, F32)
        acc_sc[...] = jnp.zeros(acc_sc.shape, F32)
        fc_sc[...] = jnp.zeros(fc_sc.shape, F32)

    q2 = q2_sc[...]
    row = lax.broadcasted_iota(jnp.int32, (rows, cols), 0)
    col = lax.broadcasted_iota(jnp.int32, (rows, cols), 1)
    own_head = _div_pow2(row, nq) == (col & (heads - 1))

    def head_scan(x, inclusive_prefix):
        lane = lax.broadcasted_iota(jnp.int32, x.shape, 1)
        k = heads
        while k < cols:
            shifted = pltpu.roll(x, k, axis=1)
            x = x + (jnp.where(lane >= k, shifted, 0.0) if inclusive_prefix else shifted)
            k *= 2
        return x

    def attend(pages_kvl, causal=None):
        n = len(pages_kvl)
        lf = jnp.concatenate([x[2] for x in pages_kvl], axis=0) if n > 1 else pages_kvl[0][2]
        cum, tot = head_scan(lf, True), head_scan(lf, False)
        base = fc_sc[...]
        ts = []
        for c, (k3, _, _) in enumerate(pages_kvl):
            f2 = (base + cum[c:c + 1]) * LOG2E
            base = base + tot[c:c + 1]
            t = jnp.where(own_head, _dot_nt(q2, k3.reshape(cols, hd).astype(BF16)) - f2, NEG_BIG)
            if causal is not None:
                t = jnp.where(causal, t, NEG_BIG)
            ts.append(t)
        fc_sc[...] = base
        m_old = m_sc[...]
        m_new = m_old
        for t in ts:
            m_new = jnp.maximum(m_new, jnp.max(t, axis=-1, keepdims=True))
        alpha = jnp.exp2(m_old - m_new)
        l_new = alpha * l_sc[...]
        acc = alpha * acc_sc[...]
        for t, (_, v3, _) in zip(ts, pages_kvl):
            p = jnp.exp2(t - m_new)
            l_new = l_new + jnp.sum(p, axis=-1, keepdims=True)
            acc = acc + _dot(p.astype(BF16), v3.reshape(cols, hd).astype(BF16))
        l_sc[...] = l_new
        acc_sc[...] = acc
        m_sc[...] = m_new

    attend([(k_refs[c][...], v_refs[c][...], lf_refs[c][...]) for c in range(pages)])

    @pl.when(j == pl.num_programs(1) - 1)
    def _():
        zeros = jnp.zeros((psz - nq, heads, hd), F32)
        causal = _div_pow2(col, heads) <= (row & (nq - 1))
        attend([(jnp.concatenate([kn_ref[...], zeros], axis=0), jnp.concatenate([vn_ref[...], zeros], axis=0),
                 lfn_ref[...])], causal)
        o = acc_sc[...] / l_sc[...]
        for h in range(heads):
            o_ref[:, h * hd:(h + 1) * hd] = o[h * nq:(h + 1) * nq].astype(o_ref.dtype)


def _attn_sample(page_table, q, k_new, v_new, lf_new, cache_k, cache_v, cache_lf, layer, *, pages):
    nb, nq, width = q.shape
    _, _, psz, heads, hd = cache_k.shape
    cols = psz * heads
    steps = page_table.shape[1] // pages
    rows = heads * nq
    per_seq = lambda shape: pl.BlockSpec((None,) + shape, lambda b, j, pt: (b,) + (0,) * len(shape))

    def page_spec(shape, c):
        return pl.BlockSpec((None, None) + shape,
                            lambda b, j, pt: (layer, pt[b, j * pages + c]) + (0,) * len(shape))

    in_specs = ([per_seq((nq, width)), per_seq((nq, heads, hd)), per_seq((nq, heads, hd)), per_seq((1, cols))]
                + [page_spec((psz, heads, hd), c) for c in range(pages)]
                + [page_spec((psz, heads, hd), c) for c in range(pages)]
                + [page_spec((1, cols), c) for c in range(pages)])
    return pl.pallas_call(
        functools.partial(_attn_sample_kernel, pages=pages, heads=heads, hd=hd, c2=hd ** -0.5 * LOG2E),
        grid_spec=pltpu.PrefetchScalarGridSpec(
            num_scalar_prefetch=1,
            grid=(nb, steps),
            in_specs=in_specs,
            out_specs=per_seq((nq, width)),
            scratch_shapes=[pltpu.VMEM((rows, hd), BF16), pltpu.VMEM((rows, 1), F32),
                            pltpu.VMEM((rows, 1), F32), pltpu.VMEM((rows, hd), F32),
                            pltpu.VMEM((1, cols), F32)]),
        out_shape=jax.ShapeDtypeStruct((nb, nq, width), F32),
        compiler_params=_cparams(("arbitrary", "arbitrary"), 40),
        name="attn_sample",
    )(page_table, q, k_new, v_new, lf_new, *([cache_k] * pages), *([cache_v] * pages), *([cache_lf] * pages))


def _ssm_kernel(u_ref, bre_ref, bim_ref, cre_ref, cim_ref, d_ref, wglu_ref, abre_ref, abim_ref,
                zre_ref, zim_ref, h0re_ref, h0im_ref, o_ref, hre_ref, him_ref,
                sre, sim, cr, ci, *, nb, lane_chunk):
    c = pl.program_id(0)
    tr, n = sre.shape
    half = SUBLANES // 2

    @pl.when(c == 0)
    def _():
        cr[...] = h0re_ref[...]
        ci[...] = h0im_ref[...]

    u = u_ref[...]
    ub = u.astype(BF16)
    bu_re, bu_im = _dot(ub, bre_ref[...]), _dot(ub, bim_ref[...])
    z_re, z_im = zre_ref[...], zim_ref[...]
    sre[...] = z_re * bu_re - z_im * bu_im
    sim[...] = z_re * bu_im + z_im * bu_re

    for lc in range(n // lane_chunk):
        lanes = slice(lc * lane_chunk, (lc + 1) * lane_chunk)
        a_re = jnp.broadcast_to(abre_ref[:, lanes], (SUBLANES, lane_chunk))
        a_im = jnp.broadcast_to(abim_ref[:, lanes], (SUBLANES, lane_chunk))
        lower = lax.broadcasted_iota(jnp.int32, (SUBLANES, lane_chunk), 0) < half

        def step(hr, hi, xr, xi):
            return a_re * hr - a_im * hi + xr, a_re * hi + a_im * hr + xi

        def body(r, carry):
            hr, hi = carry
            row = pl.multiple_of(r * SUBLANES, SUBLANES)
            xr, xi = sre[pl.ds(row, SUBLANES), lanes], sim[pl.ds(row, SUBLANES), lanes]
            if nb == SUBLANES:
                nr, ni = step(hr, hi, xr, xi)
            else:
                t1r, t1i = step(pltpu.roll(hr, half, axis=0), pltpu.roll(hi, half, axis=0), xr, xi)
                t2r, t2i = step(pltpu.roll(t1r, half, axis=0), pltpu.roll(t1i, half, axis=0), xr, xi)
                nr, ni = jnp.where(lower, t1r, t2r), jnp.where(lower, t1i, t2i)
            sre[pl.ds(row, SUBLANES), lanes] = nr
            sim[pl.ds(row, SUBLANES), lanes] = ni
            return nr, ni

        hr, hi = lax.fori_loop(0, tr // SUBLANES, body, (cr[:, lanes], ci[:, lanes]))
        cr[:, lanes] = hr
        ci[:, lanes] = hi

    y = _dot(sre[...].astype(BF16), cre_ref[...]) - _dot(sim[...].astype(BF16), cim_ref[...]) + d_ref[...] * u
    y = _gelu_tanh(y)
    o_ref[...] = (y * jax.nn.sigmoid(_dot(y.astype(BF16), wglu_ref[...]))).astype(o_ref.dtype)

    @pl.when(c == pl.num_programs(0) - 1)
    def _():
        hre_ref[...] = cr[...]
        him_ref[...] = ci[...]


def _ssm(u, bre, bim, cre, cim, dvec, wglu, ab_re, ab_im, z_re, z_im, h0_re, h0_im, *, nb, tr):
    rows, w = u.shape
    n = bre.shape[1]
    assert nb in (SUBLANES // 2, SUBLANES)
    full = lambda a: _resident(a.shape)
    lead = jnp.zeros((SUBLANES - nb, n), F32)
    h0_re, h0_im = jnp.concatenate([lead, h0_re], axis=0), jnp.concatenate([lead, h0_im], axis=0)
    o, h_re, h_im = pl.pallas_call(
        functools.partial(_ssm_kernel, nb=nb, lane_chunk=512),
        grid=(rows // tr,),
        in_specs=[pl.BlockSpec((tr, w), lambda c: (c, 0)),
                  full(bre), full(bim), full(cre), full(cim), full(dvec), full(wglu),
                  full(ab_re), full(ab_im), full(z_re), full(z_im), full(h0_re), full(h0_im)],
        out_specs=[pl.BlockSpec((tr, w), lambda c: (c, 0)),
                   pl.BlockSpec((SUBLANES, n), lambda c: (0, 0)), pl.BlockSpec((SUBLANES, n), lambda c: (0, 0))],
        out_shape=[jax.ShapeDtypeStruct((rows, w), BF16),
                   jax.ShapeDtypeStruct((SUBLANES, n), F32), jax.ShapeDtypeStruct((SUBLANES, n), F32)],
        scratch_shapes=[pltpu.VMEM((tr, n), F32), pltpu.VMEM((tr, n), F32),
                        pltpu.VMEM((SUBLANES, n), F32), pltpu.VMEM((SUBLANES, n), F32)],
        compiler_params=_cparams(("arbitrary",), 48),
        name="ssm",
    )(u, bre, bim, cre, cim, dvec, wglu, ab_re, ab_im, z_re, z_im, h0_re, h0_im)
    return o, h_re[SUBLANES - nb:], h_im[SUBLANES - nb:]


def _pool_kernel(u_ref, buf_ref, w_ref, sc_ref, o_ref, nb_ref, ext, *, rs, pos0):
    n = u_ref.shape[0]
    hdr = POOL_HDR * rs
    gw = w_ref.shape[1]
    ext[0:rs, :] = jnp.zeros((rs, ext.shape[1]), F32)
    ext[rs:hdr, :] = buf_ref[...]
    ext[hdr:hdr + n, :] = u_ref[...]
    pos = pos0 + _div_pow2(lax.broadcasted_iota(jnp.int32, (n, gw), 0), rs)
    for gi, win in enumerate(POOL_WINDOWS):
        lanes = slice(gi * gw, (gi + 1) * gw)
        tot = ext[hdr:hdr + n, lanes]
        for back in range(1, win):
            tot = tot + ext[hdr - back * rs:hdr - back * rs + n, lanes]
        cnt = jnp.minimum(pos + 1, win).astype(F32)
        pooled = tot / cnt - u_ref[:, lanes]
        o_ref[:, lanes] = (_dot(pooled.astype(BF16), w_ref[gi]) * sc_ref[:, lanes]).astype(o_ref.dtype)
    nb_ref[...] = ext[n + rs:n + hdr, :]


def _pool(u, buf, w, scale, *, blocks, rs, pos0):
    t, width = u.shape
    n = t // blocks
    hist = POOL_BUF * rs
    return pl.pallas_call(
        functools.partial(_pool_kernel, rs=rs, pos0=pos0),
        grid=(blocks,),
        in_specs=[pl.BlockSpec((n, width), lambda b: (b, 0)),
                  pl.BlockSpec((None, hist, width), lambda b: (b, 0, 0)),
                  _resident(w.shape), _resident(scale.shape)],
        out_specs=[pl.BlockSpec((n, width), lambda b: (b, 0)),
                   pl.BlockSpec((None, hist, width), lambda b: (b, 0, 0))],
        out_shape=[jax.ShapeDtypeStruct((t, width), BF16), jax.ShapeDtypeStruct((blocks, hist, width), F32)],
        scratch_shapes=[pltpu.VMEM((POOL_HDR * rs + n, width), F32)],
        compiler_params=_cparams(("arbitrary",), 48),
        name="pool",
    )(u, buf, w, scale)


def _merge_kernel(gt_ref, oa_ref, os_ref, op_ref, x_ref, gm_ref, gpost_ref, gpre_ref, scf_ref, shf_ref,
                  wb_ref, wo_ref, xo_ref, h2_ref, *, d, aw, sw):
    merged = (gt_ref[:, 0:d] * _dot(oa_ref[...], wb_ref[0:aw, :])
              + gt_ref[:, d:2 * d] * _dot(os_ref[...], wb_ref[aw:aw + sw, :])
              + gt_ref[:, 2 * d:3 * d] * _dot(op_ref[...], wb_ref[aw + sw:, :]))
    y = _dot(merged.astype(BF16), wo_ref[...])
    x = x_ref[...] + gm_ref[...] * _rms(y, gpost_ref[...])
    xo_ref[...] = x
    h2_ref[...] = (_rms(x, gpre_ref[...]) * (1.0 + scf_ref[...]) + shf_ref[...]).astype(h2_ref.dtype)


def _merge(gates, o_att, o_ssm, o_pool, x, mod, g_post, g_pre_ffn, wb, wo, *, tm, seq, time_major, aw, sw):
    t, d = x.shape
    rb = mod.shape[2]
    if time_major:
        bidx = lambda i: 0
        os_spec = pl.BlockSpec((tm, sw), lambda i: (i, 0))
    else:
        tpb = seq // tm
        bidx = lambda i: i // tpb
        os_spec = pl.BlockSpec((tm, sw), lambda i: (i % tpb, i // tpb))
    row = lambda w: pl.BlockSpec((tm, w), lambda i: (i, 0))
    modspec = lambda kk: pl.BlockSpec((None, None, rb, d), lambda i: (kk, bidx(i), 0, 0))
    return pl.pallas_call(
        functools.partial(_merge_kernel, d=d, aw=aw, sw=sw),
        grid=(t // tm,),
        in_specs=[row(3 * d), row(aw), os_spec, row(o_pool.shape[1]), row(d),
                  modspec(2), _resident((1, d)), _resident((1, d)), modspec(4), modspec(3),
                  _resident(wb.shape), _resident(wo.shape)],
        out_specs=[row(d), row(d)],
        out_shape=[jax.ShapeDtypeStruct((t, d), F32), jax.ShapeDtypeStruct((t, d), BF16)],
        compiler_params=_cparams(("arbitrary",), 56),
        name="merge",
    )(gates, o_att, o_ssm, o_pool, x, mod, g_post, g_pre_ffn, mod, mod, wb, wo)


def _ffn_up_kernel(h_ref, halo_ref, wg_ref, wv_ref, cwg_ref, cwv_ref, cbg_ref, cbv_ref, bg_ref, bv_ref,
                   act_ref, ncg_ref, ncv_ref, extg, extv, *, rs, tpb, chunk):
    i = pl.program_id(1)
    tm = h_ref.shape[0]
    back = CONV_BUF * rs
    halves = ((wg_ref, cwg_ref, cbg_ref, bg_ref, extg, ncg_ref), (wv_ref, cwv_ref, cbv_ref, bv_ref, extv, ncv_ref))

    for w_ref, _, _, buf_ref, ext, _ in halves:
        if rs == 1:
            ext[CONV_HDR - SUBLANES:CONV_HDR, :] = _dot(halo_ref[...], w_ref[...])

            @pl.when(i % tpb == 0)
            def _():
                ext[CONV_HDR - back:CONV_HDR, :] = buf_ref[...]
        else:
            ext[CONV_HDR - back:CONV_HDR, :] = buf_ref[...]

    for c in range(tm // chunk):
        r0 = CONV_HDR + c * chunk
        conv = []
        for w_ref, cw_ref, cb_ref, _, ext, _ in halves:
            ext[r0:r0 + chunk, :] = _dot(h_ref[c * chunk:(c + 1) * chunk, :], w_ref[...])
            y = cb_ref[...] + cw_ref[CONV_WIDTH - 1:CONV_WIDTH, :] * ext[r0:r0 + chunk, :]
            for tap in range(CONV_WIDTH - 1):
                off = r0 - (CONV_WIDTH - 1 - tap) * rs
                y = y + cw_ref[tap:tap + 1, :] * ext[off:off + chunk, :]
            conv.append(y)
        act_ref[c * chunk:(c + 1) * chunk, :] = (_gelu_tanh(conv[0]) * conv[1]).astype(act_ref.dtype)

    for _, _, _, _, ext, nc_ref in halves:
        nc_ref[...] = ext[CONV_HDR + tm - back:CONV_HDR + tm, :]


def _ffn_up(h2, w_up, conv_w, conv_b, buf, *, tm, seq, rs, tn):
    t, d = h2.shape
    f = w_up.shape[1] // 2
    nj = f // tn
    back = CONV_BUF * rs
    nblk = buf.shape[0]
    tpb = max(seq // tm, 1) if rs == 1 else 1
    blk = (lambda i: i // tpb) if rs == 1 else (lambda i: 0)
    halo_rows = tm // SUBLANES
    col = lambda rows, off: pl.BlockSpec((rows, tn), lambda j, i: (0, j + off))
    state = lambda off: pl.BlockSpec((None, back, tn), lambda j, i: (blk(i), 0, j + off))
    return pl.pallas_call(
        functools.partial(_ffn_up_kernel, rs=rs, tpb=tpb, chunk=min(tm, 256)),
        grid=(nj, t // tm),
        in_specs=[pl.BlockSpec((tm, d), lambda j, i: (i, 0)),
                  pl.BlockSpec((SUBLANES, d), lambda j, i: (jnp.maximum(i * halo_rows - 1, 0), 0)),
                  col(d, 0), col(d, nj), col(CONV_WIDTH, 0), col(CONV_WIDTH, nj), col(1, 0), col(1, nj),
                  state(0), state(nj)],
        out_specs=[pl.BlockSpec((tm, tn), lambda j, i: (i, j)),
                   pl.BlockSpec((None, back, tn), lambda j, i: (blk(i), 0, j)),
                   pl.BlockSpec((None, back, tn), lambda j, i: (blk(i), 0, j))],
        out_shape=[jax.ShapeDtypeStruct((t, f), BF16),
                   jax.ShapeDtypeStruct((nblk, back, f), F32), jax.ShapeDtypeStruct((nblk, back, f), F32)],
        scratch_shapes=[pltpu.VMEM((CONV_HDR + tm, tn), F32), pltpu.VMEM((CONV_HDR + tm, tn), F32)],
        compiler_params=_cparams(("arbitrary", "arbitrary"), 48),
        name="ffn_up",
    )(h2, h2, w_up, w_up, conv_w, conv_w, conv_b, conv_b, buf, buf)


def _ffn_down_kernel(act_ref, w_ref, x_ref, gf_ref, gpost_ref, o_ref):
    o_ref[...] = x_ref[...] + gf_ref[...] * _rms(_dot(act_ref[...], w_ref[...]), gpost_ref[...])


def _ffn_down(act, w_down, x, mod, g_post, *, tm, seq, time_major):
    t, d = x.shape
    f = act.shape[1]
    rb = mod.shape[2]
    tpb = 1 if time_major else seq // tm
    bidx = (lambda i: 0) if time_major else (lambda i: i // tpb)
    return pl.pallas_call(
        _ffn_down_kernel,
        grid=(t // tm,),
        in_specs=[pl.BlockSpec((tm, f), lambda i: (i, 0)), _resident(w_down.shape),
                  pl.BlockSpec((tm, d), lambda i: (i, 0)),
                  pl.BlockSpec((None, None, rb, d), lambda i: (5, bidx(i), 0, 0)), _resident((1, d))],
        out_specs=pl.BlockSpec((tm, d), lambda i: (i, 0)),
        out_shape=jax.ShapeDtypeStruct((t, d), F32),
        compiler_params=_cparams(("arbitrary",), 56),
        name="ffn_down",
    )(act, w_down, x, mod, g_post)


def _block_diag(w):
    g, r, c = w.shape
    eye = jnp.eye(g, dtype=w.dtype)
    return (w[:, :, None, :] * eye[:, None, :, None]).reshape(g * r, g * c)


def _pick(n, pref):
    return pref if n % pref == 0 else n


def _layer(x, mod, lw, attn_fn, h0_re, h0_im, pool_buf, conv_buf, *, nb, seq, time_major, pos0, dims):
    aw, sw, pw = dims["aw"], dims["sw"], dims["pw"]
    t, d = x.shape
    rs = nb if time_major else 1
    tm_in = t if time_major else _pick(seq, 256)
    h, q, k, v, u_ssm, u_pool, lf = _inproj(
        x, lw["g_pre_mix"], mod, lw["wqt"], lw["wvt"], lw["wk"], lw["wf"], lw["b_f"],
        tm=tm_in, seq=seq, nb=nb, sw=sw, pw=pw, time_major=time_major, c2=dims["c2"])
    gates = _gates(h, lw["wg"], tm=t if time_major else _pick(t, 1024))
    o_att = attn_fn(q, k, v, lf)
    o_ssm, h_re, h_im = _ssm(
        u_ssm.reshape(t, sw), lw["bre"], lw["bim"], lw["cre"], lw["cim"], lw["ssm_d"], lw["w_glu"],
        lw["ab_re"], lw["ab_im"], lw["z_re"], lw["z_im"], h0_re, h0_im, nb=nb, tr=_pick(t, 512))
    if not time_major:
        o_ssm = o_ssm.reshape(seq, nb * sw)
    o_pool, new_pool = _pool(u_pool, pool_buf, lw["pool_w"], lw["pool_scale"],
                             blocks=1 if time_major else nb, rs=rs, pos0=pos0)
    x, h2 = _merge(gates, o_att, o_ssm, o_pool, x, mod, lw["g_post_mix"], lw["g_pre_ffn"], lw["w_branch"],
                   lw["w_out"], tm=tm_in, seq=seq, time_major=time_major, aw=aw, sw=sw)
    act, ncg, ncv = _ffn_up(h2, lw["w_up"], lw["conv_w"], lw["conv_b"], conv_buf,
                            tm=t if time_major else _pick(seq, 1024), seq=seq, rs=rs, tn=dims["tn_ff"])
    x = _ffn_down(act, lw["w_down"], x, mod, lw["g_post_ffn"], tm=tm_in, seq=seq, time_major=time_major)
    return x, k, v, lf, h_re, h_im, new_pool, jnp.concatenate([ncg, ncv], axis=-1)


def kernel(x_prompt, x_sample, cache_k, cache_v, cache_logf, page_table, state_ssm_re, state_ssm_im, state_pool, state_ffn_conv, c_prompt, c_sample, w_ada, b_ada, g_pre_mix, g_post_mix, g_pre_ffn, g_post_ffn, w_in, b_f, ssm_a_re, ssm_a_im, ssm_log_step, ssm_b_re, ssm_b_im, ssm_c_re, ssm_c_im, ssm_d, w_glu, pool_w, pool_scale, w_branch, w_out, w_up, conv_w, conv_b, w_down):
    bp, seq, d = x_prompt.shape
    bs, ds, _ = x_sample.shape
    depth, n_pool, psz, heads, hd = cache_k.shape
    aw = heads * hd
    groups, nstate = ssm_a_re.shape[1], ssm_a_re.shape[2]
    sw = ssm_d.shape[1]
    pw = pool_scale.shape[1]
    f = w_down.shape[1]
    past = page_table.shape[1] * psz
    tp, ts = bp * seq, bs * ds
    off_f = 3 * aw
    off_ssm = off_f + heads
    off_gate = off_ssm + sw + pw
    dims = dict(aw=aw, sw=sw, pw=pw, tn_ff=_pick(f, 512), c2=hd ** -0.5 * LOG2E)

    wqt = jnp.swapaxes(w_in[:, :, 0:aw], 1, 2).astype(BF16)
    wvt = jnp.swapaxes(w_in[:, :, 2 * aw:off_f], 1, 2).astype(BF16)
    wk = jnp.concatenate([w_in[:, :, aw:2 * aw], w_in[:, :, off_ssm:off_gate]], axis=-1).astype(BF16)
    wf = jnp.pad(w_in[:, :, off_f:off_ssm], ((0, 0), (0, 0), (0, LANES - heads))).astype(BF16)
    bf = jnp.pad(b_f, ((0, 0), (0, LANES - heads)))[:, None, :]
    wg = w_in[:, :, off_gate:].astype(BF16)
    ab_re, ab_im, z_re, z_im = _ssm_disc(ssm_a_re, ssm_a_im, ssm_log_step)
    layers = []
    for l in range(depth):
        layers.append(dict(
            g_pre_mix=g_pre_mix[l][None], g_post_mix=g_post_mix[l][None],
            g_pre_ffn=g_pre_ffn[l][None], g_post_ffn=g_post_ffn[l][None],
            wqt=wqt[l], wvt=wvt[l], wk=wk[l], wf=wf[l], wg=wg[l], b_f=bf[l],
            bre=_block_diag(jnp.swapaxes(ssm_b_re[l], 1, 2)).astype(BF16),
            bim=_block_diag(jnp.swapaxes(ssm_b_im[l], 1, 2)).astype(BF16),
            cre=_block_diag(jnp.swapaxes(ssm_c_re[l], 1, 2)).astype(BF16),
            cim=_block_diag(jnp.swapaxes(ssm_c_im[l], 1, 2)).astype(BF16),
            ssm_d=ssm_d[l][None], w_glu=w_glu[l].astype(BF16),
            ab_re=ab_re[l][None], ab_im=ab_im[l][None], z_re=z_re[l][None], z_im=z_im[l][None],
            pool_w=pool_w[l].astype(BF16), pool_scale=pool_scale[l][None],
            w_branch=w_branch[l].astype(BF16), w_out=w_out[l].astype(BF16),
            w_up=w_up[l].astype(BF16), conv_w=conv_w[l], conv_b=conv_b[l][None],
            w_down=w_down[l].astype(BF16)))

    n_c = bp + bs
    c_rows = jnp.concatenate([c_prompt, c_sample, jnp.zeros((-n_c % 16, d), F32)], axis=0).astype(BF16)
    mod_all = _ada(c_rows, w_ada, b_ada)

    cache_lf = cache_logf.reshape(depth, n_pool, 1, psz * heads)
    pages = next(p for p in (8, 4, 2, 1) if page_table.shape[1] % p == 0)

    def to_tm(a):
        return jnp.swapaxes(a.reshape((bs, ds) + a.shape[1:]), 0, 1).reshape((ts,) + a.shape[1:])

    def from_tm(a):
        return jnp.swapaxes(a.reshape((ds, bs) + a.shape[1:]), 0, 1)

    xp = x_prompt.reshape(tp, d)
    xs = to_tm(x_sample.reshape(ts, d))
    outs_p, outs_s = [], []
    for l in range(depth):
        lw = layers[l]
        mod_p = jnp.swapaxes(mod_all[l, :bp].reshape(bp, 6, 1, d), 0, 1)

        def attn_p(qt, k, vt, lf):
            return _attn_prompt(qt, k, vt, _fcum(lf, seq, heads), nb=bp, seq=seq, heads=heads, hd=hd,
                                tq=_pick(seq, 512), hpb=2)

        zst = jnp.zeros((bp, groups * nstate), F32)
        xp, k, vt, lf, h_re, h_im, new_pool, new_conv = _layer(
            xp, mod_p, lw, attn_p, zst, zst, jnp.zeros((bp, POOL_BUF, pw), F32),
            jnp.zeros((bp, CONV_BUF, 2 * f), F32),
            nb=bp, seq=seq, time_major=False, pos0=0, dims=dims)
        outs_p.append((k.reshape(bp, seq, heads, hd), vt.T.reshape(bp, seq, heads, hd),
                       lf[:, :heads].reshape(bp, seq, heads),
                       h_re.reshape(bp, groups, nstate), h_im.reshape(bp, groups, nstate),
                       new_pool, new_conv))

        mod_s = jnp.tile(mod_all[l, bp:n_c], (ds, 1)).reshape(ts, 6, d).transpose(1, 0, 2)[:, None]

        def attn_s(q, k, v, lf, l=l):
            lf_new = from_tm(lf[:, :heads]).reshape(bs, 1, ds * heads)
            lf_new = jnp.pad(lf_new, ((0, 0), (0, 0), (0, (psz - ds) * heads)))
            o = _attn_sample(page_table, from_tm(q), from_tm(k).reshape(bs, ds, heads, hd),
                             from_tm(v).reshape(bs, ds, heads, hd), lf_new,
                             cache_k, cache_v, cache_lf, l, pages=pages)
            return to_tm(o.reshape(ts, aw)).astype(BF16)

        pool_buf_s = jnp.swapaxes(state_pool[l], 0, 1).reshape(1, POOL_BUF * bs, pw)
        conv_buf_s = jnp.swapaxes(state_ffn_conv[l], 0, 1).reshape(1, CONV_BUF * bs, 2 * f)
        xs, k, v, lf, h_re, h_im, new_pool, new_conv = _layer(
            xs, mod_s, lw, attn_s,
            state_ssm_re[l].reshape(bs, groups * nstate), state_ssm_im[l].reshape(bs, groups * nstate),
            pool_buf_s, conv_buf_s, nb=bs, seq=ds, time_major=True, pos0=past, dims=dims)
        outs_s.append((from_tm(k).reshape(bs, ds, heads, hd), from_tm(v).reshape(bs, ds, heads, hd),
                       from_tm(lf[:, :heads]),
                       h_re.reshape(bs, groups, nstate), h_im.reshape(bs, groups, nstate),
                       jnp.swapaxes(new_pool.reshape(POOL_BUF, bs, pw), 0, 1),
                       jnp.swapaxes(new_conv.reshape(CONV_BUF, bs, 2 * f), 0, 1)))

    stack = lambda outs: tuple(jnp.stack([o[i] for o in outs]) for i in range(7))
    return ((xp.reshape(bp, seq, d), from_tm(xs)) + stack(outs_p) + stack(outs_s))
```

```python
import functools
import math

import jax
import jax.numpy as jnp
from jax import lax
from jax.experimental import pallas as pl
from jax.experimental.pallas import tpu as pltpu

F32 = jnp.float32
BF16 = jnp.bfloat16

EPS = 1e-6
POOL_WINDOWS = (2, 4, 8, 16)
POOL_BUF = max(POOL_WINDOWS) - 1
POOL_HDR = POOL_BUF + 1
CONV_WIDTH = 3
CONV_BUF = CONV_WIDTH - 1
CONV_HDR = 16
SUBLANES = 8
LANES = 128
MIB = 1024 * 1024
NEG_BIG = -1e30
LOG2E = math.log2(math.e)


def _cparams(semantics, vmem_mib):
    return pltpu.CompilerParams(dimension_semantics=semantics, vmem_limit_bytes=vmem_mib * MIB)


def _resident(shape):
    nd = len(shape)
    return pl.BlockSpec(shape, lambda *_: (0,) * nd, pipeline_mode=pl.Buffered(1))


def _layer_resident(arr, layer):
    nd = arr.ndim - 1
    return pl.BlockSpec((None,) + arr.shape[1:], lambda *_: (layer,) + (0,) * nd, pipeline_mode=pl.Buffered(1))


def _dot(a, b):
    return jnp.dot(a, b, preferred_element_type=F32)


def _dot_nt(a, b):
    return lax.dot_general(a, b, (((1,), (1,)), ((), ())), preferred_element_type=F32)


def _rms(x, g):
    return x * lax.rsqrt(jnp.mean(x * x, axis=-1, keepdims=True) + EPS) * g


def _gelu_tanh(x):
    c = math.sqrt(2.0 / math.pi)
    return 0.5 * x * (1.0 + jnp.tanh(c * (x + 0.044715 * (x * x * x))))


def _log_sigmoid(x):
    return jnp.minimum(x, 0.0) - jnp.log1p(jnp.exp(-jnp.abs(x)))


def _div_pow2(x, c):
    assert c > 0 and c & (c - 1) == 0
    return x >> (c.bit_length() - 1)


def _row_cumsum(x):
    n = x.shape[0]
    row = lax.broadcasted_iota(jnp.int32, x.shape, 0)
    k = 1
    while k < n:
        x = x + jnp.where(row >= k, pltpu.roll(x, k, axis=0), 0.0)
        k *= 2
    return x


def _ada_kernel(c_ref, w_ref, b_ref, o_ref):
    o_ref[...] = _dot(c_ref[...], w_ref[...].astype(BF16)) + b_ref[...]


def _ada(c_rows, w_ada, b_ada):
    depth, d, n = w_ada.shape
    r = c_rows.shape[0]
    tn = 1024
    return pl.pallas_call(
        _ada_kernel,
        grid=(depth, n // tn),
        in_specs=[pl.BlockSpec((r, d), lambda l, j: (0, 0)),
                  pl.BlockSpec((None, d, tn), lambda l, j: (l, 0, j)),
                  pl.BlockSpec((None, 1, tn), lambda l, j: (l, 0, j))],
        out_specs=pl.BlockSpec((None, r, tn), lambda l, j: (l, 0, j)),
        out_shape=jax.ShapeDtypeStruct((depth, r, n), F32),
        compiler_params=_cparams(("arbitrary", "arbitrary"), 40),
        name="ada",
    )(c_rows, w_ada, b_ada.reshape(depth, 1, n))


def _ssm_disc_kernel(are_ref, aim_ref, ls_ref, abre_ref, abim_ref, zre_ref, zim_ref):
    a_re, a_im = are_ref[...], aim_ref[...]
    step = jnp.exp(ls_ref[...])
    mag = jnp.exp(step * a_re)
    ab_re = mag * jnp.cos(step * a_im)
    ab_im = mag * jnp.sin(step * a_im)
    den = a_re * a_re + a_im * a_im
    abre_ref[...] = ab_re
    abim_ref[...] = ab_im
    zre_ref[...] = ((ab_re - 1.0) * a_re + ab_im * a_im) / den
    zim_ref[...] = (ab_im * a_re - (ab_re - 1.0) * a_im) / den


def _ssm_disc(a_re, a_im, log_step):
    depth, g, p = a_re.shape
    n = g * p
    ls = jnp.broadcast_to(log_step[:, :, None], (depth, g, p)).reshape(depth, n)
    spec = pl.BlockSpec((depth, n), lambda: (0, 0))
    return pl.pallas_call(
        _ssm_disc_kernel,
        in_specs=[spec] * 3,
        out_specs=[spec] * 4,
        out_shape=[jax.ShapeDtypeStruct((depth, n), F32)] * 4,
        name="ssm_disc",
    )(a_re.reshape(depth, n), a_im.reshape(depth, n), ls)


def _inproj_kernel(x_ref, g_ref, sc_ref, sh_ref, wqt_ref, wvt_ref, wk_ref, wf_ref, bf_ref, *rest,
                   aw, sw, pw, hd, transposed, c2):
    h = _rms(x_ref[...], g_ref[...]) * (1.0 + sc_ref[...]) + sh_ref[...]
    hb = h.astype(BF16)
    k = _dot(hb, wk_ref[:, 0:aw])
    v = _dot_nt(hb, wvt_ref[...])
    if transposed:
        _, _, h_ref, q_ref, kb_ref, vt_ref, us_ref, up_ref, lf_ref, k5_ref, v5_ref = rest
        q_ref[...] = (_dot_nt(wqt_ref[...], hb) * c2).astype(q_ref.dtype)
        vt_ref[...] = _dot_nt(wvt_ref[...], hb).astype(vt_ref.dtype)
        kb_ref[...] = k.astype(kb_ref.dtype)
        k5_ref[...] = pltpu.einshape("htd->thd", jnp.stack([k[:, hh * hd:(hh + 1) * hd] for hh in range(aw // hd)]))
        v5_ref[...] = pltpu.einshape("htd->thd", jnp.stack([v[:, hh * hd:(hh + 1) * hd] for hh in range(aw // hd)]))
    else:
        h_ref, q_ref, k_ref, v_ref, us_ref, up_ref, lf_ref = rest
        q_ref[...] = _dot_nt(hb, wqt_ref[...])
        k_ref[...] = k
        v_ref[...] = v
    h_ref[...] = hb
    us_ref[...] = _dot(hb, wk_ref[:, aw:aw + sw])
    up_ref[...] = _dot(hb, wk_ref[:, aw + sw:aw + sw + pw])
    lf_ref[...] = _log_sigmoid(_dot(hb, wf_ref[...]) + bf_ref[...])


def _inproj(x, g, mod, lw, layer, kv_out, *, tm, seq, nb, sw, pw, hd, time_major, c2):
    t, d = x.shape
    wqt, wvt, wk, wf, bf = lw["wqt"], lw["wvt"], lw["wk"], lw["wf"], lw["b_f"]
    aw = wqt.shape[1]
    heads = aw // hd
    rb = mod.shape[2]
    row = lambda w: pl.BlockSpec((tm, w), lambda i: (i, 0))
    colT = pl.BlockSpec((aw, tm), lambda i: (0, i))
    common_in = [row(d), _layer_resident(lw["g_pre_mix"], layer)]
    weights = [_layer_resident(a, layer) for a in (wqt, wvt, wk, wf, bf)]
    sd = jax.ShapeDtypeStruct
    if time_major:
        assert tm == t
        modspec = lambda kk: pl.BlockSpec((None, None, rb, d), lambda i: (kk, 0, 0, 0))
        return pl.pallas_call(
            functools.partial(_inproj_kernel, aw=aw, sw=sw, pw=pw, hd=hd, transposed=False, c2=c2),
            grid=(1,),
            in_specs=common_in + [modspec(1), modspec(0)] + weights,
            out_specs=[row(d), row(aw), row(aw), row(aw), row(sw), row(pw), row(LANES)],
            out_shape=[sd((t, d), BF16), sd((t, aw), F32), sd((t, aw), F32), sd((t, aw), F32),
                       sd((t, sw), F32), sd((t, pw), F32), sd((t, LANES), F32)],
            compiler_params=_cparams(("arbitrary",), 48),
            name="inproj",
        )(x, lw["g_pre_mix"], mod, mod, wqt, wvt, wk, wf, bf)
    tpb = seq // tm
    modspec = lambda kk: pl.BlockSpec((None, None, rb, d), lambda i: (kk, i // tpb, 0, 0))
    slab = pl.BlockSpec((None, None, tm, heads, hd), lambda i: (layer, i // tpb, i % tpb, 0, 0))
    hbm = pl.BlockSpec(memory_space=pl.ANY)
    k5, v5 = kv_out
    n_in = 2 + 2 + len(weights)
    outs = pl.pallas_call(
        functools.partial(_inproj_kernel, aw=aw, sw=sw, pw=pw, hd=hd, transposed=True, c2=c2),
        grid=(t // tm,),
        in_specs=common_in + [modspec(1), modspec(0)] + weights + [hbm, hbm],
        out_specs=[row(d), colT, row(aw), colT, pl.BlockSpec((tm, sw), lambda i: (i % tpb, i // tpb)),
                   row(pw), row(LANES), slab, slab],
        out_shape=[sd((t, d), BF16), sd((aw, t), BF16), sd((t, aw), BF16), sd((aw, t), BF16),
                   sd((seq, nb * sw), F32), sd((t, pw), F32), sd((t, LANES), F32),
                   sd(k5.shape, F32), sd(v5.shape, F32)],
        input_output_aliases={n_in: 7, n_in + 1: 8},
        compiler_params=_cparams(("arbitrary",), 48),
        name="inproj",
    )(x, lw["g_pre_mix"], mod, mod, wqt, wvt, wk, wf, bf, k5, v5)
    return outs[:7], (outs[7], outs[8])


def _gates_kernel(h_ref, w_ref, o_ref):
    o_ref[...] = jax.nn.sigmoid(_dot(h_ref[...], w_ref[...]))


def _gates(h, wg, layer, *, tm):
    t, d = h.shape
    n = wg.shape[2]
    tn = 1024
    return pl.pallas_call(
        _gates_kernel,
        grid=(n // tn, t // tm),
        in_specs=[pl.BlockSpec((tm, d), lambda j, i: (i, 0)),
                  pl.BlockSpec((None, d, tn), lambda j, i: (layer, 0, j))],
        out_specs=pl.BlockSpec((tm, tn), lambda j, i: (i, j)),
        out_shape=jax.ShapeDtypeStruct((t, n), F32),
        compiler_params=_cparams(("arbitrary", "arbitrary"), 48),
        name="gates",
    )(h, wg)


def _fcum_kernel(lf_ref, o_ref, *, heads):
    neg = _row_cumsum(lf_ref[...]) * (-LOG2E)
    hi = neg.astype(BF16).astype(F32)
    r1 = neg - hi
    mid = r1.astype(BF16).astype(F32)
    lo = (r1 - mid).astype(BF16).astype(F32)
    lane = lax.broadcasted_iota(jnp.int32, neg.shape, 1)
    out = jnp.where(lane < heads, hi,
                    jnp.where(lane < 2 * heads, pltpu.roll(mid, heads, axis=1),
                              jnp.where(lane < 3 * heads, pltpu.roll(lo, 2 * heads, axis=1), 0.0)))
    o_ref[...] = out.astype(o_ref.dtype)


def _fcum(lf, seq, heads):
    t, w = lf.shape
    return pl.pallas_call(
        functools.partial(_fcum_kernel, heads=heads),
        grid=(t // seq,),
        in_specs=[pl.BlockSpec((seq, w), lambda b: (b, 0))],
        out_specs=pl.BlockSpec((seq, w), lambda b: (b, 0)),
        out_shape=jax.ShapeDtypeStruct((t, w), BF16),
        name="fcum",
    )(lf)


def _attn_kernel(qt_ref, k_ref, vt_ref, f_ref, o_ref, m_sc, l_sc, acc_sc, *, hd, hpb, heads):
    i, j = pl.program_id(2), pl.program_id(3)
    tq = qt_ref.shape[1]

    @pl.when(j == 0)
    def _():
        m_sc[...] = jnp.full(m_sc.shape, NEG_BIG, F32)
        l_sc[...] = jnp.zeros(l_sc.shape, F32)
        acc_sc[...] = jnp.zeros(acc_sc.shape, F32)

    def block(diagonal):
        for g in range(hpb):
            rows = slice(g * hd, (g + 1) * hd)
            head = pl.program_id(1) * hpb + g
            r = lax.broadcasted_iota(jnp.int32, (LANES, tq), 0)
            pick = jnp.where(r < 3 * heads, jnp.where((r & (heads - 1)) == head, 1.0, 0.0), 0.0).astype(BF16)
            qaug = jnp.concatenate([qt_ref[rows, :], pick], axis=0)
            kaug = jnp.concatenate([k_ref[:, rows], f_ref[...]], axis=1)
            t = _dot(kaug, qaug)
            if diagonal:
                kpos = lax.broadcasted_iota(jnp.int32, t.shape, 0)
                qpos = lax.broadcasted_iota(jnp.int32, t.shape, 1)
                t = jnp.where(kpos <= qpos, t, NEG_BIG)
            m_old = m_sc[g]
            m_new = jnp.maximum(m_old, jnp.max(t, axis=0, keepdims=True))
            alpha = jnp.exp2(m_old - m_new)
            p = jnp.exp2(t - m_new)
            l_sc[g] = alpha * l_sc[g] + jnp.sum(p, axis=0, keepdims=True)
            acc_sc[rows, :] = alpha * acc_sc[rows, :] + _dot(vt_ref[rows, :], p.astype(BF16))
            m_sc[g] = m_new

    @pl.when(j < i)
    def _():
        block(False)

    @pl.when(j == i)
    def _():
        block(True)
        for g in range(hpb):
            rows = slice(g * hd, (g + 1) * hd)
            o_ref[:, rows] = (acc_sc[rows, :] / l_sc[g]).T.astype(o_ref.dtype)


def _attn_prompt(qt, k, vt, faug, *, nb, seq, heads, hd, tq, hpb):
    t = k.shape[0]
    nq = seq // tq
    w = hpb * hd
    assert heads & (heads - 1) == 0 and 3 * heads <= LANES
    kblk = lambda b, i, j: b * nq + jnp.minimum(j, i)
    return pl.pallas_call(
        functools.partial(_attn_kernel, hd=hd, hpb=hpb, heads=heads),
        grid=(nb, heads // hpb, nq, nq),
        in_specs=[pl.BlockSpec((w, tq), lambda b, h, i, j: (h, b * nq + i)),
                  pl.BlockSpec((tq, w), lambda b, h, i, j: (kblk(b, i, j), h)),
                  pl.BlockSpec((w, tq), lambda b, h, i, j: (h, kblk(b, i, j))),
                  pl.BlockSpec((tq, LANES), lambda b, h, i, j: (kblk(b, i, j), 0))],
        out_specs=pl.BlockSpec((tq, w), lambda b, h, i, j: (b * nq + i, h)),
        out_shape=jax.ShapeDtypeStruct((t, heads * hd), BF16),
        scratch_shapes=[pltpu.VMEM((hpb, 1, tq), F32), pltpu.VMEM((hpb, 1, tq), F32),
                        pltpu.VMEM((w, tq), F32)],
        compiler_params=_cparams(("arbitrary",) * 4, 40),
        name="attn_prompt",
    )(qt, k, vt, faug)


def _attn_sample_kernel(pt_ref, q_ref, kn_ref, vn_ref, lfn_ref, *rest, pages, heads, hd, c2):
    k_refs, v_refs, lf_refs = rest[0:pages], rest[pages:2 * pages], rest[2 * pages:3 * pages]
    o_ref, q2_sc, m_sc, l_sc, acc_sc, fc_sc = rest[3 * pages:]
    del pt_ref
    j = pl.program_id(1)
    nq = q_ref.shape[0]
    rows = heads * nq
    psz = k_refs[0].shape[0]
    cols = psz * heads
    assert nq & (nq - 1) == 0 and heads & (heads - 1) == 0

    @pl.when(j == 0)
    def _():
        q2 = jnp.concatenate([q_ref[:, h * hd:(h + 1) * hd] for h in range(heads)], axis=0)
        q2_sc[...] = (q2 * c2).astype(BF16)
        m_sc[...] = jnp.full(m_sc.shape, NEG_BIG, F32)
        l_sc[...] = jnp.zeros(l_sc.shape, F32)
        acc_sc[...] = jnp.zeros(acc_sc.shape, F32)
        fc_sc[...] = jnp.zeros(fc_sc.shape, F32)

    q2 = q2_sc[...]
    row = lax.broadcasted_iota(jnp.int32, (rows, cols), 0)
    col = lax.broadcasted_iota(jnp.int32, (rows, cols), 1)
    own_head = _div_pow2(row, nq) == (col & (heads - 1))

    def head_scan(x, inclusive_prefix):
        lane = lax.broadcasted_iota(jnp.int32, x.shape, 1)
        k = heads
        while k < cols:
            shifted = pltpu.roll(x, k, axis=1)
            x = x + (jnp.where(lane >= k, shifted, 0.0) if inclusive_prefix else shifted)
            k *= 2
        return x

    def attend(pages_kvl, causal=None):
        n = len(pages_kvl)
        lf = jnp.concatenate([x[2] for x in pages_kvl], axis=0) if n > 1 else pages_kvl[0][2]
        cum, tot = head_scan(lf, True), head_scan(lf, False)
        base = fc_sc[...]
        ts = []
        for c, (k3, _, _) in enumerate(pages_kvl):
            f2 = (base + cum[c:c + 1]) * LOG2E
            base = base + tot[c:c + 1]
            t = jnp.where(own_head, _dot_nt(q2, k3.reshape(cols, hd).astype(BF16)) - f2, NEG_BIG)
            if causal is not None:
                t = jnp.where(causal, t, NEG_BIG)
            ts.append(t)
        fc_sc[...] = base
        m_old = m_sc[...]
        m_new = m_old
        for t in ts:
            m_new = jnp.maximum(m_new, jnp.max(t, axis=-1, keepdims=True))
        alpha = jnp.exp2(m_old - m_new)
        l_new = alpha * l_sc[...]
        acc = alpha * acc_sc[...]
        for t, (_, v3, _) in zip(ts, pages_kvl):
            p = jnp.exp2(t - m_new)
            l_new = l_new + jnp.sum(p, axis=-1, keepdims=True)
            acc = acc + _dot(p.astype(BF16), v3.reshape(cols, hd).astype(BF16))
        l_sc[...] = l_new
        acc_sc[...] = acc
        m_sc[...] = m_new

    attend([(k_refs[c][...], v_refs[c][...], lf_refs[c][...]) for c in range(pages)])

    @pl.when(j == pl.num_programs(1) - 1)
    def _():
        zeros = jnp.zeros((psz - nq, heads, hd), F32)
        causal = _div_pow2(col, heads) <= (row & (nq - 1))
        attend([(jnp.concatenate([kn_ref[...], zeros], axis=0), jnp.concatenate([vn_ref[...], zeros], axis=0),
                 lfn_ref[...])], causal)
        o = acc_sc[...] / l_sc[...]
        for h in range(heads):
            o_ref[:, h * hd:(h + 1) * hd] = o[h * nq:(h + 1) * nq].astype(o_ref.dtype)


def _attn_sample(page_table, q, k_new, v_new, lf_new, cache_k, cache_v, cache_lf, layer, *, pages):
    nb, nq, width = q.shape
    _, _, psz, heads, hd = cache_k.shape
    cols = psz * heads
    steps = page_table.shape[1] // pages
    rows = heads * nq
    per_seq = lambda shape: pl.BlockSpec((None,) + shape, lambda b, j, pt: (b,) + (0,) * len(shape))

    def page_spec(shape, c):
        return pl.BlockSpec((None, None) + shape,
                            lambda b, j, pt: (layer, pt[b, j * pages + c]) + (0,) * len(shape))

    in_specs = ([per_seq((nq, width)), per_seq((nq, heads, hd)), per_seq((nq, heads, hd)), per_seq((1, cols))]
                + [page_spec((psz, heads, hd), c) for c in range(pages)]
                + [page_spec((psz, heads, hd), c) for c in range(pages)]
                + [page_spec((1, cols), c) for c in range(pages)])
    return pl.pallas_call(
        functools.partial(_attn_sample_kernel, pages=pages, heads=heads, hd=hd, c2=hd ** -0.5 * LOG2E),
        grid_spec=pltpu.PrefetchScalarGridSpec(
            num_scalar_prefetch=1,
            grid=(nb, steps),
            in_specs=in_specs,
            out_specs=per_seq((nq, width)),
            scratch_shapes=[pltpu.VMEM((rows, hd), BF16), pltpu.VMEM((rows, 1), F32),
                            pltpu.VMEM((rows, 1), F32), pltpu.VMEM((rows, hd), F32),
                            pltpu.VMEM((1, cols), F32)]),
        out_shape=jax.ShapeDtypeStruct((nb, nq, width), F32),
        compiler_params=_cparams(("arbitrary", "arbitrary"), 40),
        name="attn_sample",
    )(page_table, q, k_new, v_new, lf_new, *([cache_k] * pages), *([cache_v] * pages), *([cache_lf] * pages))


def _ssm_kernel(u_ref, bre_ref, bim_ref, cre_ref, cim_ref, d_ref, wglu_ref, abre_ref, abim_ref,
                zre_ref, zim_ref, h0re_ref, h0im_ref, o_ref, hre_ref, him_ref,
                sre, sim, cr, ci, *, nb, lane_chunk):
    c = pl.program_id(0)
    tr, n = sre.shape
    half = SUBLANES // 2

    @pl.when(c == 0)
    def _():
        cr[...] = h0re_ref[...]
        ci[...] = h0im_ref[...]

    u = u_ref[...]
    ub = u.astype(BF16)
    w = u.shape[1]
    nc = LANES * (n // w)
    blocks = [(slice(kb * LANES, (kb + 1) * LANES), slice(kb * nc, (kb + 1) * nc)) for kb in range(w // LANES)]
    for ch, st in blocks:
        bu_re, bu_im = _dot(ub[:, ch], bre_ref[ch, st]), _dot(ub[:, ch], bim_ref[ch, st])
        z_re, z_im = zre_ref[:, st], zim_ref[:, st]
        sre[:, st] = z_re * bu_re - z_im * bu_im
        sim[:, st] = z_re * bu_im + z_im * bu_re

    for lc in range(n // lane_chunk):
        lanes = slice(lc * lane_chunk, (lc + 1) * lane_chunk)
        a_re = jnp.broadcast_to(abre_ref[:, lanes], (SUBLANES, lane_chunk))
        a_im = jnp.broadcast_to(abim_ref[:, lanes], (SUBLANES, lane_chunk))
        lower = lax.broadcasted_iota(jnp.int32, (SUBLANES, lane_chunk), 0) < half

        def step(hr, hi, xr, xi):
            return a_re * hr - a_im * hi + xr, a_re * hi + a_im * hr + xi

        def body(r, carry):
            hr, hi = carry
            row = pl.multiple_of(r * SUBLANES, SUBLANES)
            xr, xi = sre[pl.ds(row, SUBLANES), lanes], sim[pl.ds(row, SUBLANES), lanes]
            if nb == SUBLANES:
                nr, ni = step(hr, hi, xr, xi)
            else:
                t1r, t1i = step(pltpu.roll(hr, half, axis=0), pltpu.roll(hi, half, axis=0), xr, xi)
                t2r, t2i = step(pltpu.roll(t1r, half, axis=0), pltpu.roll(t1i, half, axis=0), xr, xi)
                nr, ni = jnp.where(lower, t1r, t2r), jnp.where(lower, t1i, t2i)
            sre[pl.ds(row, SUBLANES), lanes] = nr
            sim[pl.ds(row, SUBLANES), lanes] = ni
            return nr, ni

        hr, hi = lax.fori_loop(0, tr // SUBLANES, body, (cr[:, lanes], ci[:, lanes]))
        cr[:, lanes] = hr
        ci[:, lanes] = hi

    y = jnp.concatenate([_dot(sre[:, st].astype(BF16), cre_ref[st, ch]) - _dot(sim[:, st].astype(BF16), cim_ref[st, ch])
                         for ch, st in blocks], axis=1) + d_ref[...] * u
    y = _gelu_tanh(y)
    o_ref[...] = (y * jax.nn.sigmoid(_dot(y.astype(BF16), wglu_ref[...]))).astype(o_ref.dtype)

    @pl.when(c == pl.num_programs(0) - 1)
    def _():
        hre_ref[...] = cr[...]
        him_ref[...] = ci[...]


def _ssm(u, lw, layer, h0_re, h0_im, *, nb, tr):
    rows, w = u.shape
    assert w % LANES == 0
    names = ("bre", "bim", "cre", "cim", "ssm_d", "w_glu", "ab_re", "ab_im", "z_re", "z_im")
    n = lw["bre"].shape[2]
    assert nb in (SUBLANES // 2, SUBLANES)
    lead = jnp.zeros((SUBLANES - nb, n), F32)
    h0_re, h0_im = jnp.concatenate([lead, h0_re], axis=0), jnp.concatenate([lead, h0_im], axis=0)
    o, h_re, h_im = pl.pallas_call(
        functools.partial(_ssm_kernel, nb=nb, lane_chunk=512),
        grid=(rows // tr,),
        in_specs=[pl.BlockSpec((tr, w), lambda c: (c, 0)),
                  *[_layer_resident(lw[k], layer) for k in names],
                  _resident(h0_re.shape), _resident(h0_im.shape)],
        out_specs=[pl.BlockSpec((tr, w), lambda c: (c, 0)),
                   pl.BlockSpec((SUBLANES, n), lambda c: (0, 0)), pl.BlockSpec((SUBLANES, n), lambda c: (0, 0))],
        out_shape=[jax.ShapeDtypeStruct((rows, w), BF16),
                   jax.ShapeDtypeStruct((SUBLANES, n), F32), jax.ShapeDtypeStruct((SUBLANES, n), F32)],
        scratch_shapes=[pltpu.VMEM((tr, n), F32), pltpu.VMEM((tr, n), F32),
                        pltpu.VMEM((SUBLANES, n), F32), pltpu.VMEM((SUBLANES, n), F32)],
        compiler_params=_cparams(("arbitrary",), 48),
        name="ssm",
    )(u, *[lw[k] for k in names], h0_re, h0_im)
    return o, h_re[SUBLANES - nb:], h_im[SUBLANES - nb:]


def _pool_kernel(u_ref, buf_ref, w_ref, sc_ref, o_ref, nb_ref, ext, *, rs, pos0):
    n = u_ref.shape[0]
    hdr = POOL_HDR * rs
    gw = w_ref.shape[2]
    ext[0:rs, :] = jnp.zeros((rs, ext.shape[1]), F32)
    ext[rs:hdr, :] = buf_ref[...]
    ext[hdr:hdr + n, :] = u_ref[...]
    pos = pos0 + _div_pow2(lax.broadcasted_iota(jnp.int32, (n, gw), 0), rs)
    for gi, win in enumerate(POOL_WINDOWS):
        lanes = slice(gi * gw, (gi + 1) * gw)
        tot = ext[hdr:hdr + n, lanes]
        for back in range(1, win):
            tot = tot + ext[hdr - back * rs:hdr - back * rs + n, lanes]
        cnt = jnp.minimum(pos + 1, win).astype(F32)
        pooled = tot / cnt - u_ref[:, lanes]
        o_ref[:, lanes] = (_dot(pooled.astype(BF16), w_ref[gi]) * sc_ref[:, lanes]).astype(o_ref.dtype)
    nb_ref[...] = ext[n + rs:n + hdr, :]


def _pool(u, buf, w, scale, layer, *, blocks, rs, pos0):
    t, width = u.shape
    n = t // blocks
    hist = POOL_BUF * rs
    return pl.pallas_call(
        functools.partial(_pool_kernel, rs=rs, pos0=pos0),
        grid=(blocks,),
        in_specs=[pl.BlockSpec((n, width), lambda b: (b, 0)),
                  pl.BlockSpec((None, hist, width), lambda b: (b, 0, 0)),
                  _layer_resident(w, layer), _layer_resident(scale, layer)],
        out_specs=[pl.BlockSpec((n, width), lambda b: (b, 0)),
                   pl.BlockSpec((None, hist, width), lambda b: (b, 0, 0))],
        out_shape=[jax.ShapeDtypeStruct((t, width), BF16), jax.ShapeDtypeStruct((blocks, hist, width), F32)],
        scratch_shapes=[pltpu.VMEM((POOL_HDR * rs + n, width), F32)],
        compiler_params=_cparams(("arbitrary",), 48),
        name="pool",
    )(u, buf, w, scale)


def _merge_kernel(gt_ref, oa_ref, os_ref, op_ref, x_ref, gm_ref, gpost_ref, gpre_ref, scf_ref, shf_ref,
                  wb_ref, wo_ref, xo_ref, h2_ref, *, d, aw, sw):
    merged = (gt_ref[:, 0:d] * _dot(oa_ref[...], wb_ref[0:aw, :])
              + gt_ref[:, d:2 * d] * _dot(os_ref[...], wb_ref[aw:aw + sw, :])
              + gt_ref[:, 2 * d:3 * d] * _dot(op_ref[...], wb_ref[aw + sw:, :]))
    y = _dot(merged.astype(BF16), wo_ref[...])
    x = x_ref[...] + gm_ref[...] * _rms(y, gpost_ref[...])
    xo_ref[...] = x
    h2_ref[...] = (_rms(x, gpre_ref[...]) * (1.0 + scf_ref[...]) + shf_ref[...]).astype(h2_ref.dtype)


def _merge(gates, o_att, o_ssm, o_pool, x, mod, g_post, g_pre_ffn, wb, wo, layer, *, tm, seq, time_major,
           aw, sw):
    t, d = x.shape
    rb = mod.shape[2]
    if time_major:
        bidx = lambda i: 0
        os_spec = pl.BlockSpec((tm, sw), lambda i: (i, 0))
    else:
        tpb = seq // tm
        bidx = lambda i: i // tpb
        os_spec = pl.BlockSpec((tm, sw), lambda i: (i % tpb, i // tpb))
    row = lambda w: pl.BlockSpec((tm, w), lambda i: (i, 0))
    modspec = lambda kk: pl.BlockSpec((None, None, rb, d), lambda i: (kk, bidx(i), 0, 0))
    return pl.pallas_call(
        functools.partial(_merge_kernel, d=d, aw=aw, sw=sw),
        grid=(t // tm,),
        in_specs=[row(3 * d), row(aw), os_spec, row(o_pool.shape[1]), row(d),
                  modspec(2), _layer_resident(g_post, layer), _layer_resident(g_pre_ffn, layer),
                  modspec(4), modspec(3), _layer_resident(wb, layer), _layer_resident(wo, layer)],
        out_specs=[row(d), row(d)],
        out_shape=[jax.ShapeDtypeStruct((t, d), F32), jax.ShapeDtypeStruct((t, d), BF16)],
        compiler_params=_cparams(("arbitrary",), 56),
        name="merge",
    )(gates, o_att, o_ssm, o_pool, x, mod, g_post, g_pre_ffn, mod, mod, wb, wo)


def _ffn_up_kernel(h_ref, halo_ref, wg_ref, wv_ref, cwg_ref, cwv_ref, cbg_ref, cbv_ref, bg_ref, bv_ref,
                   act_ref, ncg_ref, ncv_ref, extg, extv, wgb, wvb, *, rs, tpb, chunk):
    i = pl.program_id(1)
    tm = h_ref.shape[0]
    back = CONV_BUF * rs

    @pl.when(i == 0)
    def _():
        wgb[...] = wg_ref[...].astype(BF16)
        wvb[...] = wv_ref[...].astype(BF16)

    halves = ((wgb, cwg_ref, cbg_ref, bg_ref, extg, ncg_ref), (wvb, cwv_ref, cbv_ref, bv_ref, extv, ncv_ref))

    for w_ref, _, _, buf_ref, ext, _ in halves:
        if rs == 1:
            ext[CONV_HDR - SUBLANES:CONV_HDR, :] = _dot(halo_ref[...], w_ref[...])

            @pl.when(i % tpb == 0)
            def _():
                ext[CONV_HDR - back:CONV_HDR, :] = buf_ref[...]
        else:
            ext[CONV_HDR - back:CONV_HDR, :] = buf_ref[...]

    for c in range(tm // chunk):
        r0 = CONV_HDR + c * chunk
        conv = []
        for w_ref, cw_ref, cb_ref, _, ext, _ in halves:
            up = _dot(h_ref[c * chunk:(c + 1) * chunk, :], w_ref[...])
            ext[r0:r0 + chunk, :] = up
            y = cb_ref[...] + cw_ref[CONV_WIDTH - 1:CONV_WIDTH, :] * up
            for tap in range(CONV_WIDTH - 1):
                off = r0 - (CONV_WIDTH - 1 - tap) * rs
                y = y + cw_ref[tap:tap + 1, :] * ext[off:off + chunk, :]
            conv.append(y)
        act_ref[c * chunk:(c + 1) * chunk, :] = (_gelu_tanh(conv[0]) * conv[1]).astype(act_ref.dtype)

    for _, _, _, _, ext, nc_ref in halves:
        nc_ref[...] = ext[CONV_HDR + tm - back:CONV_HDR + tm, :]


def _ffn_up(h2, w_up, conv_w, conv_b, layer, buf, *, tm, seq, rs, tn):
    t, d = h2.shape
    f = w_up.shape[2] // 2
    nj = f // tn
    back = CONV_BUF * rs
    nblk = buf.shape[0]
    tpb = max(seq // tm, 1) if rs == 1 else 1
    blk = (lambda i: i // tpb) if rs == 1 else (lambda i: 0)
    halo_rows = tm // SUBLANES
    col = lambda rows, off: pl.BlockSpec((None, rows, tn), lambda j, i: (layer, 0, j + off))
    state = lambda off: pl.BlockSpec((None, back, tn), lambda j, i: (blk(i), 0, j + off))
    return pl.pallas_call(
        functools.partial(_ffn_up_kernel, rs=rs, tpb=tpb, chunk=min(tm, 256)),
        grid=(nj, t // tm),
        in_specs=[pl.BlockSpec((tm, d), lambda j, i: (i, 0)),
                  pl.BlockSpec((SUBLANES, d), lambda j, i: (jnp.maximum(i * halo_rows - 1, 0), 0)),
                  col(d, 0), col(d, nj), col(CONV_WIDTH, 0), col(CONV_WIDTH, nj), col(1, 0), col(1, nj),
                  state(0), state(nj)],
        out_specs=[pl.BlockSpec((tm, tn), lambda j, i: (i, j)),
                   pl.BlockSpec((None, back, tn), lambda j, i: (blk(i), 0, j)),
                   pl.BlockSpec((None, back, tn), lambda j, i: (blk(i), 0, j))],
        out_shape=[jax.ShapeDtypeStruct((t, f), BF16),
                   jax.ShapeDtypeStruct((nblk, back, f), F32), jax.ShapeDtypeStruct((nblk, back, f), F32)],
        scratch_shapes=[pltpu.VMEM((CONV_HDR + tm, tn), F32), pltpu.VMEM((CONV_HDR + tm, tn), F32),
                        pltpu.VMEM((d, tn), BF16), pltpu.VMEM((d, tn), BF16)],
        compiler_params=_cparams(("arbitrary", "arbitrary"), 56),
        name="ffn_up",
    )(h2, h2, w_up, w_up, conv_w, conv_w, conv_b, conv_b, buf, buf)


def _ffn_down_kernel(act_ref, w_ref, x_ref, gf_ref, gpost_ref, o_ref):
    o_ref[...] = x_ref[...] + gf_ref[...] * _rms(_dot(act_ref[...], w_ref[...]), gpost_ref[...])


def _ffn_down(act, w_down, x, mod, g_post, layer, *, tm, seq, time_major):
    t, d = x.shape
    f = act.shape[1]
    rb = mod.shape[2]
    tpb = 1 if time_major else seq // tm
    bidx = (lambda i: 0) if time_major else (lambda i: i // tpb)
    return pl.pallas_call(
        _ffn_down_kernel,
        grid=(t // tm,),
        in_specs=[pl.BlockSpec((tm, f), lambda i: (i, 0)), _layer_resident(w_down, layer),
                  pl.BlockSpec((tm, d), lambda i: (i, 0)),
                  pl.BlockSpec((None, None, rb, d), lambda i: (5, bidx(i), 0, 0)),
                  _layer_resident(g_post, layer)],
        out_specs=pl.BlockSpec((tm, d), lambda i: (i, 0)),
        out_shape=jax.ShapeDtypeStruct((t, d), F32),
        compiler_params=_cparams(("arbitrary",), 56),
        name="ffn_down",
    )(act, w_down, x, mod, g_post)


def _block_diag(w):
    depth, g, r, c = w.shape
    eye = jnp.eye(g, dtype=w.dtype)
    return (w[:, :, :, None, :] * eye[None, :, None, :, None]).reshape(depth, g * r, g * c)


def _pick(n, pref):
    return pref if n % pref == 0 else n


def _layer(x, mod, lw, layer, attn_fn, kv_out, h0_re, h0_im, pool_buf, conv_buf, *, nb, seq, time_major, pos0,
           dims):
    aw, sw, pw = dims["aw"], dims["sw"], dims["pw"]
    t, d = x.shape
    rs = nb if time_major else 1
    tm_in = t if time_major else _pick(seq, 256)
    res = _inproj(x, lw["g_pre_mix"], mod, lw, layer, kv_out, tm=tm_in, seq=seq, nb=nb, sw=sw, pw=pw,
                  hd=dims["hd"], time_major=time_major, c2=dims["c2"])
    (h, q, k, v, u_ssm, u_pool, lf), kv_out = res if not time_major else (res, None)
    gates = _gates(h, lw["wg"], layer, tm=t if time_major else _pick(t, 1024))
    o_att = attn_fn(q, k, v, lf)
    o_ssm, h_re, h_im = _ssm(u_ssm.reshape(t, sw), lw, layer, h0_re, h0_im, nb=nb, tr=_pick(t, 512))
    if not time_major:
        o_ssm = o_ssm.reshape(seq, nb * sw)
    o_pool, new_pool = _pool(u_pool, pool_buf, lw["pool_w"], lw["pool_scale"], layer,
                             blocks=1 if time_major else nb, rs=rs, pos0=pos0)
    x, h2 = _merge(gates, o_att, o_ssm, o_pool, x, mod, lw["g_post_mix"], lw["g_pre_ffn"], lw["w_branch"],
                   lw["w_out"], layer, tm=tm_in, seq=seq, time_major=time_major, aw=aw, sw=sw)
    act, ncg, ncv = _ffn_up(h2, lw["w_up"], lw["conv_w"], lw["conv_b"], layer, conv_buf,
                            tm=t if time_major else _pick(seq, 1024), seq=seq, rs=rs, tn=dims["tn_ff"])
    x = _ffn_down(act, lw["w_down"], x, mod, lw["g_post_ffn"], layer, tm=tm_in, seq=seq, time_major=time_major)
    return x, (k, v, kv_out), lf, h_re, h_im, new_pool, jnp.concatenate([ncg, ncv], axis=-1)


def kernel(x_prompt, x_sample, cache_k, cache_v, cache_logf, page_table, state_ssm_re, state_ssm_im, state_pool, state_ffn_conv, c_prompt, c_sample, w_ada, b_ada, g_pre_mix, g_post_mix, g_pre_ffn, g_post_ffn, w_in, b_f, ssm_a_re, ssm_a_im, ssm_log_step, ssm_b_re, ssm_b_im, ssm_c_re, ssm_c_im, ssm_d, w_glu, pool_w, pool_scale, w_branch, w_out, w_up, conv_w, conv_b, w_down):
    bp, seq, d = x_prompt.shape
    bs, ds, _ = x_sample.shape
    depth, n_pool, psz, heads, hd = cache_k.shape
    aw = heads * hd
    groups, nstate = ssm_a_re.shape[1], ssm_a_re.shape[2]
    sw = ssm_d.shape[1]
    pw = pool_scale.shape[1]
    f = w_down.shape[1]
    past = page_table.shape[1] * psz
    tp, ts = bp * seq, bs * ds
    off_f = 3 * aw
    off_ssm = off_f + heads
    off_gate = off_ssm + sw + pw
    dims = dict(aw=aw, sw=sw, pw=pw, hd=hd, tn_ff=_pick(f, 512), c2=hd ** -0.5 * LOG2E)
    ab_re, ab_im, z_re, z_im = _ssm_disc(ssm_a_re, ssm_a_im, ssm_log_step)
    lw = dict(
        g_pre_mix=g_pre_mix[:, None], g_post_mix=g_post_mix[:, None],
        g_pre_ffn=g_pre_ffn[:, None], g_post_ffn=g_post_ffn[:, None],
        wqt=jnp.swapaxes(w_in[:, :, 0:aw], 1, 2).astype(BF16),
        wvt=jnp.swapaxes(w_in[:, :, 2 * aw:off_f], 1, 2).astype(BF16),
        wk=jnp.concatenate([w_in[:, :, aw:2 * aw], w_in[:, :, off_ssm:off_gate]], axis=-1).astype(BF16),
        wf=jnp.pad(w_in[:, :, off_f:off_ssm], ((0, 0), (0, 0), (0, LANES - heads))).astype(BF16),
        b_f=jnp.pad(b_f, ((0, 0), (0, LANES - heads)))[:, None, :],
        wg=w_in[:, :, off_gate:].astype(BF16),
        bre=_block_diag(jnp.swapaxes(ssm_b_re, 2, 3)).astype(BF16),
        bim=_block_diag(jnp.swapaxes(ssm_b_im, 2, 3)).astype(BF16),
        cre=_block_diag(jnp.swapaxes(ssm_c_re, 2, 3)).astype(BF16),
        cim=_block_diag(jnp.swapaxes(ssm_c_im, 2, 3)).astype(BF16),
        ssm_d=ssm_d[:, None], w_glu=w_glu.astype(BF16),
        ab_re=ab_re[:, None], ab_im=ab_im[:, None], z_re=z_re[:, None], z_im=z_im[:, None],
        pool_w=pool_w.astype(BF16), pool_scale=pool_scale[:, None],
        w_branch=w_branch.astype(BF16), w_out=w_out.astype(BF16),
        w_up=w_up, conv_w=conv_w, conv_b=conv_b[:, None], w_down=w_down.astype(BF16))

    n_c = bp + bs
    c_rows = jnp.concatenate([c_prompt, c_sample, jnp.zeros((-n_c % 16, d), F32)], axis=0).astype(BF16)
    mod_all = _ada(c_rows, w_ada, b_ada)

    cache_lf = cache_logf.reshape(depth, n_pool, 1, psz * heads)
    pages = next(p for p in (8, 4, 2, 1) if page_table.shape[1] % p == 0)

    def to_tm(a):
        return jnp.swapaxes(a.reshape((bs, ds) + a.shape[1:]), 0, 1).reshape((ts,) + a.shape[1:])

    def from_tm(a):
        return jnp.swapaxes(a.reshape((ds, bs) + a.shape[1:]), 0, 1)

    xp = x_prompt.reshape(tp, d)
    xs = to_tm(x_sample.reshape(ts, d))
    kv_p = (jnp.zeros((depth, bp, seq, heads, hd), F32), jnp.zeros((depth, bp, seq, heads, hd), F32))
    outs_p, outs_s = [], []
    for l in range(depth):
        mod_p = jnp.swapaxes(mod_all[l, :bp].reshape(bp, 6, 1, d), 0, 1)

        def attn_p(qt, k, vt, lf):
            return _attn_prompt(qt, k, vt, _fcum(lf, seq, heads), nb=bp, seq=seq, heads=heads, hd=hd,
                                tq=_pick(seq, 512), hpb=4)

        zst = jnp.zeros((bp, groups * nstate), F32)
        xp, (_, _, kv_p), lf, h_re, h_im, new_pool, new_conv = _layer(
            xp, mod_p, lw, l, attn_p, kv_p, zst, zst, jnp.zeros((bp, POOL_BUF, pw), F32),
            jnp.zeros((bp, CONV_BUF, 2 * f), F32),
            nb=bp, seq=seq, time_major=False, pos0=0, dims=dims)
        outs_p.append((lf[:, :heads].reshape(bp, seq, heads),
                       h_re.reshape(bp, groups, nstate), h_im.reshape(bp, groups, nstate),
                       new_pool, new_conv))

        mod_s = jnp.tile(mod_all[l, bp:n_c], (ds, 1)).reshape(ts, 6, d).transpose(1, 0, 2)[:, None]

        def attn_s(q, k, v, lf, l=l):
            lf_new = from_tm(lf[:, :heads]).reshape(bs, 1, ds * heads)
            lf_new = jnp.pad(lf_new, ((0, 0), (0, 0), (0, (psz - ds) * heads)))
            o = _attn_sample(page_table, from_tm(q), from_tm(k).reshape(bs, ds, heads, hd),
                             from_tm(v).reshape(bs, ds, heads, hd), lf_new,
                             cache_k, cache_v, cache_lf, l, pages=pages)
            return to_tm(o.reshape(ts, aw)).astype(BF16)

        pool_buf_s = jnp.swapaxes(state_pool[l], 0, 1).reshape(1, POOL_BUF * bs, pw)
        conv_buf_s = jnp.swapaxes(state_ffn_conv[l], 0, 1).reshape(1, CONV_BUF * bs, 2 * f)
        xs, (k, v, _), lf, h_re, h_im, new_pool, new_conv = _layer(
            xs, mod_s, lw, l, attn_s, None,
            state_ssm_re[l].reshape(bs, groups * nstate), state_ssm_im[l].reshape(bs, groups * nstate),
            pool_buf_s, conv_buf_s, nb=bs, seq=ds, time_major=True, pos0=past, dims=dims)
        outs_s.append((from_tm(k).reshape(bs, ds, heads, hd), from_tm(v).reshape(bs, ds, heads, hd),
                       from_tm(lf[:, :heads]),
                       h_re.reshape(bs, groups, nstate), h_im.reshape(bs, groups, nstate),
                       jnp.swapaxes(new_pool.reshape(POOL_BUF, bs, pw), 0, 1),
                       jnp.swapaxes(new_conv.reshape(CONV_BUF, bs, 2 * f), 0, 1)))

    stack = lambda outs: tuple(jnp.stack([o[i] for o in outs]) for i in range(len(outs[0])))
    return ((xp.reshape(bp, seq, d), from_tm(xs)) + kv_p + stack(outs_p) + stack(outs_s))
```

```python
import functools
import math

import jax
import jax.numpy as jnp
from jax import lax
from jax.experimental import pallas as pl
from jax.experimental.pallas import tpu as pltpu

F32 = jnp.float32
BF16 = jnp.bfloat16

EPS = 1e-6
POOL_WINDOWS = (2, 4, 8, 16)
POOL_BUF = max(POOL_WINDOWS) - 1
POOL_HDR = POOL_BUF + 1
CONV_WIDTH = 3
CONV_BUF = CONV_WIDTH - 1
CONV_HDR = 16
SUBLANES = 8
LANES = 128
MIB = 1024 * 1024
NEG_BIG = -1e30
LOG2E = math.log2(math.e)


def _cparams(semantics, vmem_mib):
    return pltpu.CompilerParams(dimension_semantics=semantics, vmem_limit_bytes=vmem_mib * MIB)


def _resident(shape):
    nd = len(shape)
    return pl.BlockSpec(shape, lambda *_: (0,) * nd, pipeline_mode=pl.Buffered(1))


def _layer_resident(arr, layer):
    nd = arr.ndim - 1
    return pl.BlockSpec((None,) + arr.shape[1:], lambda *_: (layer,) + (0,) * nd, pipeline_mode=pl.Buffered(1))


def _dot(a, b):
    return jnp.dot(a, b, preferred_element_type=F32)


def _dot_nt(a, b):
    return lax.dot_general(a, b, (((1,), (1,)), ((), ())), preferred_element_type=F32)


def _rms(x, g):
    return x * lax.rsqrt(jnp.mean(x * x, axis=-1, keepdims=True) + EPS) * g


def _gelu_tanh(x):
    c = math.sqrt(2.0 / math.pi)
    return 0.5 * x * (1.0 + jnp.tanh(c * (x + 0.044715 * (x * x * x))))


def _log_sigmoid(x):
    return jnp.minimum(x, 0.0) - jnp.log1p(jnp.exp(-jnp.abs(x)))


def _div_pow2(x, c):
    assert c > 0 and c & (c - 1) == 0
    return x >> (c.bit_length() - 1)


def _row_cumsum(x):
    n = x.shape[0]
    row = lax.broadcasted_iota(jnp.int32, x.shape, 0)
    k = 1
    while k < n:
        x = x + jnp.where(row >= k, pltpu.roll(x, k, axis=0), 0.0)
        k *= 2
    return x


def _ada_kernel(c_ref, w_ref, b_ref, o_ref):
    o_ref[...] = _dot(c_ref[...], w_ref[...].astype(BF16)) + b_ref[...]


def _ada(c_rows, w_ada, b_ada):
    depth, d, n = w_ada.shape
    r = c_rows.shape[0]
    tn = 1024
    return pl.pallas_call(
        _ada_kernel,
        grid=(depth, n // tn),
        in_specs=[pl.BlockSpec((r, d), lambda l, j: (0, 0)),
                  pl.BlockSpec((None, d, tn), lambda l, j: (l, 0, j)),
                  pl.BlockSpec((None, 1, tn), lambda l, j: (l, 0, j))],
        out_specs=pl.BlockSpec((None, r, tn), lambda l, j: (l, 0, j)),
        out_shape=jax.ShapeDtypeStruct((depth, r, n), F32),
        compiler_params=_cparams(("arbitrary", "arbitrary"), 40),
        name="ada",
    )(c_rows, w_ada, b_ada.reshape(depth, 1, n))


def _repack_kernel(w_ref, qkv_ref, f_ref, rest_ref, *, off_f, heads):
    qkv_ref[...] = w_ref[:, 0:off_f].astype(BF16)
    lane = lax.broadcasted_iota(jnp.int32, f_ref.shape, 1)
    f_ref[...] = jnp.where(lane < heads, w_ref[:, off_f:off_f + LANES], 0.0).astype(BF16)
    rest_ref[...] = w_ref[:, off_f + heads:].astype(BF16)


def _repack(w_in, off_f, heads):
    depth, d, n = w_in.shape
    tr = _pick(d, 256)
    nrest = n - off_f - heads
    return pl.pallas_call(
        functools.partial(_repack_kernel, off_f=off_f, heads=heads),
        grid=(depth, d // tr),
        in_specs=[pl.BlockSpec((None, tr, n), lambda l, i: (l, i, 0))],
        out_specs=[pl.BlockSpec((None, tr, off_f), lambda l, i: (l, i, 0)),
                   pl.BlockSpec((None, tr, LANES), lambda l, i: (l, i, 0)),
                   pl.BlockSpec((None, tr, nrest), lambda l, i: (l, i, 0))],
        out_shape=[jax.ShapeDtypeStruct((depth, d, off_f), BF16), jax.ShapeDtypeStruct((depth, d, LANES), BF16),
                   jax.ShapeDtypeStruct((depth, d, nrest), BF16)],
        compiler_params=_cparams(("arbitrary", "arbitrary"), 48),
        name="repack",
    )(w_in)


def _ssm_disc_kernel(are_ref, aim_ref, ls_ref, abre_ref, abim_ref, zre_ref, zim_ref):
    a_re, a_im = are_ref[...], aim_ref[...]
    step = jnp.exp(ls_ref[...])
    mag = jnp.exp(step * a_re)
    ab_re = mag * jnp.cos(step * a_im)
    ab_im = mag * jnp.sin(step * a_im)
    den = a_re * a_re + a_im * a_im
    abre_ref[...] = ab_re
    abim_ref[...] = ab_im
    zre_ref[...] = ((ab_re - 1.0) * a_re + ab_im * a_im) / den
    zim_ref[...] = (ab_im * a_re - (ab_re - 1.0) * a_im) / den


def _ssm_disc(a_re, a_im, log_step):
    depth, g, p = a_re.shape
    n = g * p
    ls = jnp.broadcast_to(log_step[:, :, None], (depth, g, p)).reshape(depth, n)
    spec = pl.BlockSpec((depth, n), lambda: (0, 0))
    return pl.pallas_call(
        _ssm_disc_kernel,
        in_specs=[spec] * 3,
        out_specs=[spec] * 4,
        out_shape=[jax.ShapeDtypeStruct((depth, n), F32)] * 4,
        name="ssm_disc",
    )(a_re.reshape(depth, n), a_im.reshape(depth, n), ls)


def _inproj_kernel(x_ref, g_ref, sc_ref, sh_ref, wqt_ref, wvt_ref, wk_ref, wsp_ref, wf_ref, bf_ref, *rest,
                   aw, sw, pw, hd, transposed, c2):
    h = _rms(x_ref[...], g_ref[...]) * (1.0 + sc_ref[...]) + sh_ref[...]
    hb = h.astype(BF16)
    k = _dot(hb, wk_ref[...])
    v = _dot_nt(hb, wvt_ref[...])
    if transposed:
        _, _, h_ref, q_ref, kb_ref, vt_ref, us_ref, up_ref, lf_ref, k5_ref, v5_ref = rest
        q_ref[...] = (_dot_nt(wqt_ref[...], hb) * c2).astype(q_ref.dtype)
        vt_ref[...] = _dot_nt(wvt_ref[...], hb).astype(vt_ref.dtype)
        kb_ref[...] = k.astype(kb_ref.dtype)
        k5_ref[...] = pltpu.einshape("htd->thd", jnp.stack([k[:, hh * hd:(hh + 1) * hd] for hh in range(aw // hd)]))
        v5_ref[...] = pltpu.einshape("htd->thd", jnp.stack([v[:, hh * hd:(hh + 1) * hd] for hh in range(aw // hd)]))
    else:
        h_ref, q_ref, k_ref, v_ref, us_ref, up_ref, lf_ref = rest
        q_ref[...] = _dot_nt(hb, wqt_ref[...])
        k_ref[...] = k
        v_ref[...] = v
    h_ref[...] = hb
    us_ref[...] = _dot(hb, wsp_ref[:, 0:sw])
    up_ref[...] = _dot(hb, wsp_ref[:, sw:sw + pw])
    lf_ref[...] = _log_sigmoid(_dot(hb, wf_ref[...]) + bf_ref[...])


def _inproj(x, g, mod, lw, layer, kv_out, *, tm, seq, nb, sw, pw, hd, time_major, c2):
    t, d = x.shape
    wqt, wvt, wqkv, wrest, wf, bf = lw["wqt"], lw["wvt"], lw["wqkv"], lw["wrest"], lw["wf"], lw["b_f"]
    aw = wqt.shape[1]
    assert (sw + pw) % LANES == 0
    heads = aw // hd
    rb = mod.shape[2]
    row = lambda w: pl.BlockSpec((tm, w), lambda i: (i, 0))
    colT = pl.BlockSpec((aw, tm), lambda i: (0, i))
    common_in = [row(d), _layer_resident(lw["g_pre_mix"], layer)]
    once = dict(pipeline_mode=pl.Buffered(1))
    weights = [_layer_resident(wqt, layer), _layer_resident(wvt, layer),
               pl.BlockSpec((None, d, aw), lambda *_: (layer, 0, 1), **once),
               pl.BlockSpec((None, d, sw + pw), lambda *_: (layer, 0, 0), **once),
               _layer_resident(wf, layer), _layer_resident(bf, layer)]
    sd = jax.ShapeDtypeStruct
    if time_major:
        assert tm == t
        modspec = lambda kk: pl.BlockSpec((None, None, rb, d), lambda i: (kk, 0, 0, 0))
        return pl.pallas_call(
            functools.partial(_inproj_kernel, aw=aw, sw=sw, pw=pw, hd=hd, transposed=False, c2=c2),
            grid=(1,),
            in_specs=common_in + [modspec(1), modspec(0)] + weights,
            out_specs=[row(d), row(aw), row(aw), row(aw), row(sw), row(pw), row(LANES)],
            out_shape=[sd((t, d), BF16), sd((t, aw), F32), sd((t, aw), F32), sd((t, aw), F32),
                       sd((t, sw), F32), sd((t, pw), F32), sd((t, LANES), F32)],
            compiler_params=_cparams(("arbitrary",), 48),
            name="inproj",
        )(x, lw["g_pre_mix"], mod, mod, wqt, wvt, wqkv, wrest, wf, bf)
    tpb = seq // tm
    modspec = lambda kk: pl.BlockSpec((None, None, rb, d), lambda i: (kk, i // tpb, 0, 0))
    slab = pl.BlockSpec((None, None, tm, heads, hd), lambda i: (layer, i // tpb, i % tpb, 0, 0))
    hbm = pl.BlockSpec(memory_space=pl.ANY)
    k5, v5 = kv_out
    n_in = 2 + 2 + len(weights)
    outs = pl.pallas_call(
        functools.partial(_inproj_kernel, aw=aw, sw=sw, pw=pw, hd=hd, transposed=True, c2=c2),
        grid=(t // tm,),
        in_specs=common_in + [modspec(1), modspec(0)] + weights + [hbm, hbm],
        out_specs=[row(d), colT, row(aw), colT, pl.BlockSpec((tm, sw), lambda i: (i % tpb, i // tpb)),
                   row(pw), row(LANES), slab, slab],
        out_shape=[sd((t, d), BF16), sd((aw, t), BF16), sd((t, aw), BF16), sd((aw, t), BF16),
                   sd((seq, nb * sw), F32), sd((t, pw), F32), sd((t, LANES), F32),
                   sd(k5.shape, F32), sd(v5.shape, F32)],
        input_output_aliases={n_in: 7, n_in + 1: 8},
        compiler_params=_cparams(("arbitrary",), 48),
        name="inproj",
    )(x, lw["g_pre_mix"], mod, mod, wqt, wvt, wqkv, wrest, wf, bf, k5, v5)
    return outs[:7], (outs[7], outs[8])


def _gates_kernel(h_ref, w_ref, o_ref):
    o_ref[...] = jax.nn.sigmoid(_dot(h_ref[...], w_ref[...]))


def _gates(h, wrest, layer, *, tm, skip):
    t, d = h.shape
    n = wrest.shape[2] - skip
    tn = 1024
    assert skip % tn == 0
    return pl.pallas_call(
        _gates_kernel,
        grid=(n // tn, t // tm),
        in_specs=[pl.BlockSpec((tm, d), lambda j, i: (i, 0)),
                  pl.BlockSpec((None, d, tn), lambda j, i: (layer, 0, skip // tn + j))],
        out_specs=pl.BlockSpec((tm, tn), lambda j, i: (i, j)),
        out_shape=jax.ShapeDtypeStruct((t, n), F32),
        compiler_params=_cparams(("arbitrary", "arbitrary"), 48),
        name="gates",
    )(h, wrest)


def _fcum_kernel(lf_ref, o_ref, *, heads):
    neg = _row_cumsum(lf_ref[...]) * (-LOG2E)
    hi = neg.astype(BF16).astype(F32)
    r1 = neg - hi
    mid = r1.astype(BF16).astype(F32)
    lo = (r1 - mid).astype(BF16).astype(F32)
    lane = lax.broadcasted_iota(jnp.int32, neg.shape, 1)
    out = jnp.where(lane < heads, hi,
                    jnp.where(lane < 2 * heads, pltpu.roll(mid, heads, axis=1),
                              jnp.where(lane < 3 * heads, pltpu.roll(lo, 2 * heads, axis=1), 0.0)))
    o_ref[...] = out.astype(o_ref.dtype)


def _fcum(lf, seq, heads):
    t, w = lf.shape
    return pl.pallas_call(
        functools.partial(_fcum_kernel, heads=heads),
        grid=(t // seq,),
        in_specs=[pl.BlockSpec((seq, w), lambda b: (b, 0))],
        out_specs=pl.BlockSpec((seq, w), lambda b: (b, 0)),
        out_shape=jax.ShapeDtypeStruct((t, w), BF16),
        name="fcum",
    )(lf)


def _attn_kernel(qt_ref, k_ref, vt_ref, f_ref, o_ref, m_sc, l_sc, acc_sc, *, hd, hpb, heads):
    i, j = pl.program_id(2), pl.program_id(3)
    tq = qt_ref.shape[1]

    @pl.when(j == 0)
    def _():
        m_sc[...] = jnp.full(m_sc.shape, NEG_BIG, F32)
        l_sc[...] = jnp.zeros(l_sc.shape, F32)
        acc_sc[...] = jnp.zeros(acc_sc.shape, F32)

    def block(diagonal):
        for g in range(hpb):
            rows = slice(g * hd, (g + 1) * hd)
            head = pl.program_id(1) * hpb + g
            r = lax.broadcasted_iota(jnp.int32, (LANES, tq), 0)
            pick = jnp.where(r < 3 * heads, jnp.where((r & (heads - 1)) == head, 1.0, 0.0), 0.0).astype(BF16)
            qaug = jnp.concatenate([qt_ref[rows, :], pick], axis=0)
            kaug = jnp.concatenate([k_ref[:, rows], f_ref[...]], axis=1)
            t = _dot(kaug, qaug)
            if diagonal:
                kpos = lax.broadcasted_iota(jnp.int32, t.shape, 0)
                qpos = lax.broadcasted_iota(jnp.int32, t.shape, 1)
                t = jnp.where(kpos <= qpos, t, NEG_BIG)
            m_old = m_sc[g]
            m_new = jnp.maximum(m_old, jnp.max(t, axis=0, keepdims=True))
            alpha = jnp.exp2(m_old - m_new)
            p = jnp.exp2(t - m_new)
            l_sc[g] = alpha * l_sc[g] + jnp.sum(p, axis=0, keepdims=True)
            acc_sc[rows, :] = alpha * acc_sc[rows, :] + _dot(vt_ref[rows, :], p.astype(BF16))
            m_sc[g] = m_new

    @pl.when(j < i)
    def _():
        block(False)

    @pl.when(j == i)
    def _():
        block(True)
        for g in range(hpb):
            rows = slice(g * hd, (g + 1) * hd)
            o_ref[:, rows] = (acc_sc[rows, :] / l_sc[g]).T.astype(o_ref.dtype)


def _attn_prompt(qt, k, vt, faug, *, nb, seq, heads, hd, tq, hpb):
    t = k.shape[0]
    nq = seq // tq
    w = hpb * hd
    assert heads & (heads - 1) == 0 and 3 * heads <= LANES
    kblk = lambda b, i, j: b * nq + jnp.minimum(j, i)
    return pl.pallas_call(
        functools.partial(_attn_kernel, hd=hd, hpb=hpb, heads=heads),
        grid=(nb, heads // hpb, nq, nq),
        in_specs=[pl.BlockSpec((w, tq), lambda b, h, i, j: (h, b * nq + i)),
                  pl.BlockSpec((tq, w), lambda b, h, i, j: (kblk(b, i, j), h)),
                  pl.BlockSpec((w, tq), lambda b, h, i, j: (h, kblk(b, i, j))),
                  pl.BlockSpec((tq, LANES), lambda b, h, i, j: (kblk(b, i, j), 0))],
        out_specs=pl.BlockSpec((tq, w), lambda b, h, i, j: (b * nq + i, h)),
        out_shape=jax.ShapeDtypeStruct((t, heads * hd), BF16),
        scratch_shapes=[pltpu.VMEM((hpb, 1, tq), F32), pltpu.VMEM((hpb, 1, tq), F32),
                        pltpu.VMEM((w, tq), F32)],
        compiler_params=_cparams(("arbitrary",) * 4, 40),
        name="attn_prompt",
    )(qt, k, vt, faug)


def _attn_sample_kernel(pt_ref, q_ref, kn_ref, vn_ref, lfn_ref, *rest, pages, heads, hd, c2):
    k_refs, v_refs, lf_refs = rest[0:pages], rest[pages:2 * pages], rest[2 * pages:3 * pages]
    o_ref, q2_sc, m_sc, l_sc, acc_sc, fc_sc = rest[3 * pages:]
    del pt_ref
    j = pl.program_id(1)
    nq = q_ref.shape[0]
    rows = heads * nq
    psz = k_refs[0].shape[0]
    cols = psz * heads
    assert nq & (nq - 1) == 0 and heads & (heads - 1) == 0

    @pl.when(j == 0)
    def _():
        q2 = jnp.concatenate([q_ref[:, h * hd:(h + 1) * hd] for h in range(heads)], axis=0)
        q2_sc[...] = (q2 * c2).astype(BF16)
        m_sc[...] = jnp.full(m_sc.shape, NEG_BIG, F32)
        l_sc[...] = jnp.zeros(l_sc.shape, F32)
        acc_sc[...] = jnp.zeros(acc_sc.shape, F32)
        fc_sc[...] = jnp.zeros(fc_sc.shape, F32)

    q2 = q2_sc[...]
    row = lax.broadcasted_iota(jnp.int32, (rows, cols), 0)
    col = lax.broadcasted_iota(jnp.int32, (rows, cols), 1)
    own_head = _div_pow2(row, nq) == (col & (heads - 1))

    def head_scan(x, inclusive_prefix):
        lane = lax.broadcasted_iota(jnp.int32, x.shape, 1)
        k = heads
        while k < cols:
            shifted = pltpu.roll(x, k, axis=1)
            x = x + (jnp.where(lane >= k, shifted, 0.0) if inclusive_prefix else shifted)
            k *= 2
        return x

    def attend(pages_kvl, causal=None):
        n = len(pages_kvl)
        lf = jnp.concatenate([x[2] for x in pages_kvl], axis=0) if n > 1 else pages_kvl[0][2]
        cum, tot = head_scan(lf, True), head_scan(lf, False)
        base = fc_sc[...]
        ts = []
        for c, (k3, _, _) in enumerate(pages_kvl):
            f2 = (base + cum[c:c + 1]) * LOG2E
            base = base + tot[c:c + 1]
            t = jnp.where(own_head, _dot_nt(q2, k3.reshape(cols, hd).astype(BF16)) - f2, NEG_BIG)
            if causal is not None:
                t = jnp.where(causal, t, NEG_BIG)
            ts.append(t)
        fc_sc[...] = base
        m_old = m_sc[...]
        m_new = m_old
        for t in ts:
            m_new = jnp.maximum(m_new, jnp.max(t, axis=-1, keepdims=True))
        alpha = jnp.exp2(m_old - m_new)
        l_new = alpha * l_sc[...]
        acc = alpha * acc_sc[...]
        for t, (_, v3, _) in zip(ts, pages_kvl):
            p = jnp.exp2(t - m_new)
            l_new = l_new + jnp.sum(p, axis=-1, keepdims=True)
            acc = acc + _dot(p.astype(BF16), v3.reshape(cols, hd).astype(BF16))
        l_sc[...] = l_new
        acc_sc[...] = acc
        m_sc[...] = m_new

    attend([(k_refs[c][...], v_refs[c][...], lf_refs[c][...]) for c in range(pages)])

    @pl.when(j == pl.num_programs(1) - 1)
    def _():
        zeros = jnp.zeros((psz - nq, heads, hd), F32)
        causal = _div_pow2(col, heads) <= (row & (nq - 1))
        attend([(jnp.concatenate([kn_ref[...], zeros], axis=0), jnp.concatenate([vn_ref[...], zeros], axis=0),
                 lfn_ref[...])], causal)
        o = acc_sc[...] / l_sc[...]
        for h in range(heads):
            o_ref[:, h * hd:(h + 1) * hd] = o[h * nq:(h + 1) * nq].astype(o_ref.dtype)


def _attn_sample(page_table, q, k_new, v_new, lf_new, cache_k, cache_v, cache_lf, layer, *, pages):
    nb, nq, width = q.shape
    _, _, psz, heads, hd = cache_k.shape
    cols = psz * heads
    steps = page_table.shape[1] // pages
    rows = heads * nq
    per_seq = lambda shape: pl.BlockSpec((None,) + shape, lambda b, j, pt: (b,) + (0,) * len(shape))

    def page_spec(shape, c):
        return pl.BlockSpec((None, None) + shape,
                            lambda b, j, pt: (layer, pt[b, j * pages + c]) + (0,) * len(shape))

    in_specs = ([per_seq((nq, width)), per_seq((nq, heads, hd)), per_seq((nq, heads, hd)), per_seq((1, cols))]
                + [page_spec((psz, heads, hd), c) for c in range(pages)]
                + [page_spec((psz, heads, hd), c) for c in range(pages)]
                + [page_spec((1, cols), c) for c in range(pages)])
    return pl.pallas_call(
        functools.partial(_attn_sample_kernel, pages=pages, heads=heads, hd=hd, c2=hd ** -0.5 * LOG2E),
        grid_spec=pltpu.PrefetchScalarGridSpec(
            num_scalar_prefetch=1,
            grid=(nb, steps),
            in_specs=in_specs,
            out_specs=per_seq((nq, width)),
            scratch_shapes=[pltpu.VMEM((rows, hd), BF16), pltpu.VMEM((rows, 1), F32),
                            pltpu.VMEM((rows, 1), F32), pltpu.VMEM((rows, hd), F32),
                            pltpu.VMEM((1, cols), F32)]),
        out_shape=jax.ShapeDtypeStruct((nb, nq, width), F32),
        compiler_params=_cparams(("arbitrary", "arbitrary"), 52),
        name="attn_sample",
    )(page_table, q, k_new, v_new, lf_new, *([cache_k] * pages), *([cache_v] * pages), *([cache_lf] * pages))


def _ssm_kernel(u_ref, bre_ref, bim_ref, cre_ref, cim_ref, d_ref, wglu_ref, abre_ref, abim_ref,
                zre_ref, zim_ref, h0re_ref, h0im_ref, o_ref, hre_ref, him_ref,
                sre, sim, cr, ci, *, nb, lane_chunk):
    c = pl.program_id(0)
    tr, n = sre.shape
    half = SUBLANES // 2

    @pl.when(c == 0)
    def _():
        cr[...] = h0re_ref[...]
        ci[...] = h0im_ref[...]

    u = u_ref[...]
    ub = u.astype(BF16)
    w = u.shape[1]
    nc = LANES * (n // w)
    blocks = [(kb, slice(kb * LANES, (kb + 1) * LANES), slice(kb * nc, (kb + 1) * nc)) for kb in range(w // LANES)]
    for kb, ch, st in blocks:
        bu_re, bu_im = _dot(ub[:, ch], bre_ref[kb]), _dot(ub[:, ch], bim_ref[kb])
        z_re, z_im = zre_ref[:, st], zim_ref[:, st]
        sre[:, st] = z_re * bu_re - z_im * bu_im
        sim[:, st] = z_re * bu_im + z_im * bu_re

    for lc in range(n // lane_chunk):
        lanes = slice(lc * lane_chunk, (lc + 1) * lane_chunk)
        a_re = jnp.broadcast_to(abre_ref[:, lanes], (SUBLANES, lane_chunk))
        a_im = jnp.broadcast_to(abim_ref[:, lanes], (SUBLANES, lane_chunk))
        lower = lax.broadcasted_iota(jnp.int32, (SUBLANES, lane_chunk), 0) < half

        def step(hr, hi, xr, xi):
            return a_re * hr - a_im * hi + xr, a_re * hi + a_im * hr + xi

        def body(r, carry):
            hr, hi = carry
            row = pl.multiple_of(r * SUBLANES, SUBLANES)
            xr, xi = sre[pl.ds(row, SUBLANES), lanes], sim[pl.ds(row, SUBLANES), lanes]
            if nb == SUBLANES:
                nr, ni = step(hr, hi, xr, xi)
            else:
                t1r, t1i = step(pltpu.roll(hr, half, axis=0), pltpu.roll(hi, half, axis=0), xr, xi)
                t2r, t2i = step(pltpu.roll(t1r, half, axis=0), pltpu.roll(t1i, half, axis=0), xr, xi)
                nr, ni = jnp.where(lower, t1r, t2r), jnp.where(lower, t1i, t2i)
            sre[pl.ds(row, SUBLANES), lanes] = nr
            sim[pl.ds(row, SUBLANES), lanes] = ni
            return nr, ni

        hr, hi = lax.fori_loop(0, tr // SUBLANES, body, (cr[:, lanes], ci[:, lanes]))
        cr[:, lanes] = hr
        ci[:, lanes] = hi

    y = jnp.concatenate([_dot(sre[:, st].astype(BF16), cre_ref[kb]) - _dot(sim[:, st].astype(BF16), cim_ref[kb])
                         for kb, _, st in blocks], axis=1) + d_ref[...] * u
    y = _gelu_tanh(y)
    o_ref[...] = (y * jax.nn.sigmoid(_dot(y.astype(BF16), wglu_ref[...]))).astype(o_ref.dtype)

    @pl.when(c == pl.num_programs(0) - 1)
    def _():
        hre_ref[...] = cr[...]
        him_ref[...] = ci[...]


def _ssm(u, lw, layer, h0_re, h0_im, *, nb, tr):
    rows, w = u.shape
    assert w % LANES == 0
    names = ("bre", "bim", "cre", "cim", "ssm_d", "w_glu", "ab_re", "ab_im", "z_re", "z_im")
    n = lw["ab_re"].shape[2]
    assert nb in (SUBLANES // 2, SUBLANES)
    lead = jnp.zeros((SUBLANES - nb, n), F32)
    h0_re, h0_im = jnp.concatenate([lead, h0_re], axis=0), jnp.concatenate([lead, h0_im], axis=0)
    o, h_re, h_im = pl.pallas_call(
        functools.partial(_ssm_kernel, nb=nb, lane_chunk=512),
        grid=(rows // tr,),
        in_specs=[pl.BlockSpec((tr, w), lambda c: (c, 0)),
                  *[_layer_resident(lw[k], layer) for k in names],
                  _resident(h0_re.shape), _resident(h0_im.shape)],
        out_specs=[pl.BlockSpec((tr, w), lambda c: (c, 0)),
                   pl.BlockSpec((SUBLANES, n), lambda c: (0, 0)), pl.BlockSpec((SUBLANES, n), lambda c: (0, 0))],
        out_shape=[jax.ShapeDtypeStruct((rows, w), BF16),
                   jax.ShapeDtypeStruct((SUBLANES, n), F32), jax.ShapeDtypeStruct((SUBLANES, n), F32)],
        scratch_shapes=[pltpu.VMEM((tr, n), F32), pltpu.VMEM((tr, n), F32),
                        pltpu.VMEM((SUBLANES, n), F32), pltpu.VMEM((SUBLANES, n), F32)],
        compiler_params=_cparams(("arbitrary",), 48),
        name="ssm",
    )(u, *[lw[k] for k in names], h0_re, h0_im)
    return o, h_re[SUBLANES - nb:], h_im[SUBLANES - nb:]


def _pool_kernel(u_ref, buf_ref, w_ref, sc_ref, o_ref, nb_ref, ext, *, rs, pos0):
    n = u_ref.shape[0]
    hdr = POOL_HDR * rs
    gw = w_ref.shape[2]
    ext[0:rs, :] = jnp.zeros((rs, ext.shape[1]), F32)
    ext[rs:hdr, :] = buf_ref[...]
    ext[hdr:hdr + n, :] = u_ref[...]
    pos = pos0 + _div_pow2(lax.broadcasted_iota(jnp.int32, (n, gw), 0), rs)
    for gi, win in enumerate(POOL_WINDOWS):
        lanes = slice(gi * gw, (gi + 1) * gw)
        tot = ext[hdr:hdr + n, lanes]
        for back in range(1, win):
            tot = tot + ext[hdr - back * rs:hdr - back * rs + n, lanes]
        cnt = jnp.minimum(pos + 1, win).astype(F32)
        pooled = tot / cnt - u_ref[:, lanes]
        o_ref[:, lanes] = (_dot(pooled.astype(BF16), w_ref[gi]) * sc_ref[:, lanes]).astype(o_ref.dtype)
    nb_ref[...] = ext[n + rs:n + hdr, :]


def _pool(u, buf, w, scale, layer, *, blocks, rs, pos0):
    t, width = u.shape
    n = t // blocks
    hist = POOL_BUF * rs
    return pl.pallas_call(
        functools.partial(_pool_kernel, rs=rs, pos0=pos0),
        grid=(blocks,),
        in_specs=[pl.BlockSpec((n, width), lambda b: (b, 0)),
                  pl.BlockSpec((None, hist, width), lambda b: (b, 0, 0)),
                  _layer_resident(w, layer), _layer_resident(scale, layer)],
        out_specs=[pl.BlockSpec((n, width), lambda b: (b, 0)),
                   pl.BlockSpec((None, hist, width), lambda b: (b, 0, 0))],
        out_shape=[jax.ShapeDtypeStruct((t, width), BF16), jax.ShapeDtypeStruct((blocks, hist, width), F32)],
        scratch_shapes=[pltpu.VMEM((POOL_HDR * rs + n, width), F32)],
        compiler_params=_cparams(("arbitrary",), 48),
        name="pool",
    )(u, buf, w, scale)


def _merge_kernel(gt_ref, oa_ref, os_ref, op_ref, x_ref, gm_ref, gpost_ref, gpre_ref, scf_ref, shf_ref,
                  wb_ref, wo_ref, xo_ref, h2_ref, *, d, aw, sw):
    merged = (gt_ref[:, 0:d] * _dot(oa_ref[...], wb_ref[0:aw, :])
              + gt_ref[:, d:2 * d] * _dot(os_ref[...], wb_ref[aw:aw + sw, :])
              + gt_ref[:, 2 * d:3 * d] * _dot(op_ref[...], wb_ref[aw + sw:, :]))
    y = _dot(merged.astype(BF16), wo_ref[...])
    x = x_ref[...] + gm_ref[...] * _rms(y, gpost_ref[...])
    xo_ref[...] = x
    h2_ref[...] = (_rms(x, gpre_ref[...]) * (1.0 + scf_ref[...]) + shf_ref[...]).astype(h2_ref.dtype)


def _merge(gates, o_att, o_ssm, o_pool, x, mod, g_post, g_pre_ffn, wb, wo, layer, *, tm, seq, time_major,
           aw, sw):
    t, d = x.shape
    rb = mod.shape[2]
    if time_major:
        bidx = lambda i: 0
        os_spec = pl.BlockSpec((tm, sw), lambda i: (i, 0))
    else:
        tpb = seq // tm
        bidx = lambda i: i // tpb
        os_spec = pl.BlockSpec((tm, sw), lambda i: (i % tpb, i // tpb))
    row = lambda w: pl.BlockSpec((tm, w), lambda i: (i, 0))
    modspec = lambda kk: pl.BlockSpec((None, None, rb, d), lambda i: (kk, bidx(i), 0, 0))
    return pl.pallas_call(
        functools.partial(_merge_kernel, d=d, aw=aw, sw=sw),
        grid=(t // tm,),
        in_specs=[row(3 * d), row(aw), os_spec, row(o_pool.shape[1]), row(d),
                  modspec(2), _layer_resident(g_post, layer), _layer_resident(g_pre_ffn, layer),
                  modspec(4), modspec(3), _layer_resident(wb, layer), _layer_resident(wo, layer)],
        out_specs=[row(d), row(d)],
        out_shape=[jax.ShapeDtypeStruct((t, d), F32), jax.ShapeDtypeStruct((t, d), BF16)],
        compiler_params=_cparams(("arbitrary",), 56),
        name="merge",
    )(gates, o_att, o_ssm, o_pool, x, mod, g_post, g_pre_ffn, mod, mod, wb, wo)


def _ffn_up_kernel(h_ref, halo_ref, wg_ref, wv_ref, cwg_ref, cwv_ref, cbg_ref, cbv_ref, bg_ref, bv_ref,
                   act_ref, ncg_ref, ncv_ref, ext, wb, *, rs, tpb):
    i = pl.program_id(1)
    tm = h_ref.shape[0]
    tn = wg_ref.shape[1]
    back = CONV_BUF * rs

    @pl.when(i == 0)
    def _():
        wb[:, 0:tn] = wg_ref[...].astype(BF16)
        wb[:, tn:2 * tn] = wv_ref[...].astype(BF16)

    if rs == 1:
        ext[CONV_HDR - SUBLANES:CONV_HDR, :] = _dot(halo_ref[...], wb[...])

        @pl.when(i % tpb == 0)
        def _():
            ext[CONV_HDR - back:CONV_HDR, 0:tn] = bg_ref[...]
            ext[CONV_HDR - back:CONV_HDR, tn:2 * tn] = bv_ref[...]
    else:
        ext[CONV_HDR - back:CONV_HDR, 0:tn] = bg_ref[...]
        ext[CONV_HDR - back:CONV_HDR, tn:2 * tn] = bv_ref[...]

    cw = jnp.concatenate([cwg_ref[...], cwv_ref[...]], axis=1)
    cb = jnp.concatenate([cbg_ref[...], cbv_ref[...]], axis=1)
    up = _dot(h_ref[...], wb[...])
    ext[CONV_HDR:CONV_HDR + tm, :] = up
    y = cb + cw[CONV_WIDTH - 1:CONV_WIDTH, :] * up
    for tap in range(CONV_WIDTH - 1):
        off = CONV_HDR - (CONV_WIDTH - 1 - tap) * rs
        y = y + cw[tap:tap + 1, :] * ext[off:off + tm, :]
    act_ref[...] = (_gelu_tanh(y[:, 0:tn]) * y[:, tn:2 * tn]).astype(act_ref.dtype)

    ncg_ref[...] = ext[CONV_HDR + tm - back:CONV_HDR + tm, 0:tn]
    ncv_ref[...] = ext[CONV_HDR + tm - back:CONV_HDR + tm, tn:2 * tn]


def _ffn_up(h2, w_up, conv_w, conv_b, layer, buf, *, tm, seq, rs, tn):
    t, d = h2.shape
    f = w_up.shape[2] // 2
    nj = f // tn
    back = CONV_BUF * rs
    nblk = buf.shape[0]
    tpb = max(seq // tm, 1) if rs == 1 else 1
    blk = (lambda i: i // tpb) if rs == 1 else (lambda i: 0)
    halo_rows = tm // SUBLANES
    col = lambda rows, off: pl.BlockSpec((None, rows, tn), lambda j, i: (layer, 0, j + off))
    state = lambda off: pl.BlockSpec((None, back, tn), lambda j, i: (blk(i), 0, j + off))
    return pl.pallas_call(
        functools.partial(_ffn_up_kernel, rs=rs, tpb=tpb),
        grid=(nj, t // tm),
        in_specs=[pl.BlockSpec((tm, d), lambda j, i: (i, 0)),
                  pl.BlockSpec((SUBLANES, d), lambda j, i: (jnp.maximum(i * halo_rows - 1, 0), 0)),
                  col(d, 0), col(d, nj), col(CONV_WIDTH, 0), col(CONV_WIDTH, nj), col(1, 0), col(1, nj),
                  state(0), state(nj)],
        out_specs=[pl.BlockSpec((tm, tn), lambda j, i: (i, j)),
                   pl.BlockSpec((None, back, tn), lambda j, i: (blk(i), 0, j)),
                   pl.BlockSpec((None, back, tn), lambda j, i: (blk(i), 0, j))],
        out_shape=[jax.ShapeDtypeStruct((t, f), BF16),
                   jax.ShapeDtypeStruct((nblk, back, f), F32), jax.ShapeDtypeStruct((nblk, back, f), F32)],
        scratch_shapes=[pltpu.VMEM((CONV_HDR + tm, 2 * tn), F32), pltpu.VMEM((d, 2 * tn), BF16)],
        compiler_params=_cparams(("arbitrary", "arbitrary"), 56),
        name="ffn_up",
    )(h2, h2, w_up, w_up, conv_w, conv_w, conv_b, conv_b, buf, buf)


def _ffn_down_kernel(act_ref, w_ref, x_ref, gf_ref, gpost_ref, o_ref):
    o_ref[...] = x_ref[...] + gf_ref[...] * _rms(_dot(act_ref[...], w_ref[...]), gpost_ref[...])


def _ffn_down(act, w_down, x, mod, g_post, layer, *, tm, seq, time_major):
    t, d = x.shape
    f = act.shape[1]
    rb = mod.shape[2]
    tpb = 1 if time_major else seq // tm
    bidx = (lambda i: 0) if time_major else (lambda i: i // tpb)
    return pl.pallas_call(
        _ffn_down_kernel,
        grid=(t // tm,),
        in_specs=[pl.BlockSpec((tm, f), lambda i: (i, 0)), _layer_resident(w_down, layer),
                  pl.BlockSpec((tm, d), lambda i: (i, 0)),
                  pl.BlockSpec((None, None, rb, d), lambda i: (5, bidx(i), 0, 0)),
                  _layer_resident(g_post, layer)],
        out_specs=pl.BlockSpec((tm, d), lambda i: (i, 0)),
        out_shape=jax.ShapeDtypeStruct((t, d), F32),
        compiler_params=_cparams(("arbitrary",), 56),
        name="ffn_down",
    )(act, w_down, x, mod, g_post)


def _block_diag(w, per):
    depth, g, r, c = w.shape
    eye = jnp.eye(per, dtype=w.dtype)
    w = w.reshape(depth, g // per, per, r, c)
    return (w[:, :, :, :, None, :] * eye[None, None, :, None, :, None]).reshape(depth, g // per, per * r, per * c)


def _pick(n, pref):
    return pref if n % pref == 0 else n


def _layer(x, mod, lw, layer, attn_fn, kv_out, h0_re, h0_im, pool_buf, conv_buf, *, nb, seq, time_major, pos0,
           dims):
    aw, sw, pw = dims["aw"], dims["sw"], dims["pw"]
    t, d = x.shape
    rs = nb if time_major else 1
    tm_in = t if time_major else _pick(seq, 256)
    res = _inproj(x, lw["g_pre_mix"], mod, lw, layer, kv_out, tm=tm_in, seq=seq, nb=nb, sw=sw, pw=pw,
                  hd=dims["hd"], time_major=time_major, c2=dims["c2"])
    (h, q, k, v, u_ssm, u_pool, lf), kv_out = res if not time_major else (res, None)
    gates = _gates(h, lw["wrest"], layer, tm=t if time_major else _pick(t, 1024), skip=sw + pw)
    o_att = attn_fn(q, k, v, lf)
    o_ssm, h_re, h_im = _ssm(u_ssm.reshape(t, sw), lw, layer, h0_re, h0_im, nb=nb, tr=_pick(t, 512))
    if not time_major:
        o_ssm = o_ssm.reshape(seq, nb * sw)
    o_pool, new_pool = _pool(u_pool, pool_buf, lw["pool_w"], lw["pool_scale"], layer,
                             blocks=1 if time_major else nb, rs=rs, pos0=pos0)
    x, h2 = _merge(gates, o_att, o_ssm, o_pool, x, mod, lw["g_post_mix"], lw["g_pre_ffn"], lw["w_branch"],
                   lw["w_out"], layer, tm=tm_in, seq=seq, time_major=time_major, aw=aw, sw=sw)
    act, ncg, ncv = _ffn_up(h2, lw["w_up"], lw["conv_w"], lw["conv_b"], layer, conv_buf,
                            tm=t if time_major else _pick(seq, 1024), seq=seq, rs=rs, tn=dims["tn_ff"])
    x = _ffn_down(act, lw["w_down"], x, mod, lw["g_post_ffn"], layer, tm=tm_in, seq=seq, time_major=time_major)
    return x, (k, v, kv_out), lf, h_re, h_im, new_pool, jnp.concatenate([ncg, ncv], axis=-1)


def kernel(x_prompt, x_sample, cache_k, cache_v, cache_logf, page_table, state_ssm_re, state_ssm_im, state_pool, state_ffn_conv, c_prompt, c_sample, w_ada, b_ada, g_pre_mix, g_post_mix, g_pre_ffn, g_post_ffn, w_in, b_f, ssm_a_re, ssm_a_im, ssm_log_step, ssm_b_re, ssm_b_im, ssm_c_re, ssm_c_im, ssm_d, w_glu, pool_w, pool_scale, w_branch, w_out, w_up, conv_w, conv_b, w_down):
    bp, seq, d = x_prompt.shape
    bs, ds, _ = x_sample.shape
    depth, n_pool, psz, heads, hd = cache_k.shape
    aw = heads * hd
    groups, nstate = ssm_a_re.shape[1], ssm_a_re.shape[2]
    sw = ssm_d.shape[1]
    pw = pool_scale.shape[1]
    f = w_down.shape[1]
    past = page_table.shape[1] * psz
    tp, ts = bp * seq, bs * ds
    off_f = 3 * aw
    dims = dict(aw=aw, sw=sw, pw=pw, hd=hd, tn_ff=_pick(f, 512), c2=hd ** -0.5 * LOG2E)
    ab_re, ab_im, z_re, z_im = _ssm_disc(ssm_a_re, ssm_a_im, ssm_log_step)
    wqkv, wf, wrest = _repack(w_in, off_f, heads)
    per = LANES // (sw // groups)
    lw = dict(
        g_pre_mix=g_pre_mix[:, None], g_post_mix=g_post_mix[:, None],
        g_pre_ffn=g_pre_ffn[:, None], g_post_ffn=g_post_ffn[:, None],
        wqt=jnp.swapaxes(wqkv[:, :, 0:aw], 1, 2), wvt=jnp.swapaxes(wqkv[:, :, 2 * aw:off_f], 1, 2),
        wqkv=wqkv, wrest=wrest, wf=wf,
        b_f=jnp.pad(b_f, ((0, 0), (0, LANES - heads)))[:, None, :],
        bre=_block_diag(jnp.swapaxes(ssm_b_re, 2, 3), per).astype(BF16),
        bim=_block_diag(jnp.swapaxes(ssm_b_im, 2, 3), per).astype(BF16),
        cre=_block_diag(jnp.swapaxes(ssm_c_re, 2, 3), per).astype(BF16),
        cim=_block_diag(jnp.swapaxes(ssm_c_im, 2, 3), per).astype(BF16),
        ssm_d=ssm_d[:, None], w_glu=w_glu.astype(BF16),
        ab_re=ab_re[:, None], ab_im=ab_im[:, None], z_re=z_re[:, None], z_im=z_im[:, None],
        pool_w=pool_w.astype(BF16), pool_scale=pool_scale[:, None],
        w_branch=w_branch.astype(BF16), w_out=w_out.astype(BF16),
        w_up=w_up, conv_w=conv_w, conv_b=conv_b[:, None], w_down=w_down.astype(BF16))

    n_c = bp + bs
    c_rows = jnp.concatenate([c_prompt, c_sample, jnp.zeros((-n_c % 16, d), F32)], axis=0).astype(BF16)
    mod_all = _ada(c_rows, w_ada, b_ada)

    cache_lf = cache_logf.reshape(depth, n_pool, 1, psz * heads)
    pages = next(p for p in (16, 8, 4, 2, 1) if page_table.shape[1] % p == 0)

    def to_tm(a):
        return jnp.swapaxes(a.reshape((bs, ds) + a.shape[1:]), 0, 1).reshape((ts,) + a.shape[1:])

    def from_tm(a):
        return jnp.swapaxes(a.reshape((ds, bs) + a.shape[1:]), 0, 1)

    xp = x_prompt.reshape(tp, d)
    xs = to_tm(x_sample.reshape(ts, d))
    kv_p = (jnp.zeros((depth, bp, seq, heads, hd), F32), jnp.zeros((depth, bp, seq, heads, hd), F32))
    outs_p, outs_s = [], []
    for l in range(depth):
        mod_p = jnp.swapaxes(mod_all[l, :bp].reshape(bp, 6, 1, d), 0, 1)

        def attn_p(qt, k, vt, lf):
            return _attn_prompt(qt, k, vt, _fcum(lf, seq, heads), nb=bp, seq=seq, heads=heads, hd=hd,
                                tq=_pick(seq, 512), hpb=4)

        zst = jnp.zeros((bp, groups * nstate), F32)
        xp, (_, _, kv_p), lf, h_re, h_im, new_pool, new_conv = _layer(
            xp, mod_p, lw, l, attn_p, kv_p, zst, zst, jnp.zeros((bp, POOL_BUF, pw), F32),
            jnp.zeros((bp, CONV_BUF, 2 * f), F32),
            nb=bp, seq=seq, time_major=False, pos0=0, dims=dims)
        outs_p.append((lf[:, :heads].reshape(bp, seq, heads),
                       h_re.reshape(bp, groups, nstate), h_im.reshape(bp, groups, nstate),
                       new_pool, new_conv))

        mod_s = jnp.tile(mod_all[l, bp:n_c], (ds, 1)).reshape(ts, 6, d).transpose(1, 0, 2)[:, None]

        def attn_s(q, k, v, lf, l=l):
            lf_new = from_tm(lf[:, :heads]).reshape(bs, 1, ds * heads)
            lf_new = jnp.pad(lf_new, ((0, 0), (0, 0), (0, (psz - ds) * heads)))
            o = _attn_sample(page_table, from_tm(q), from_tm(k).reshape(bs, ds, heads, hd),
                             from_tm(v).reshape(bs, ds, heads, hd), lf_new,
                             cache_k, cache_v, cache_lf, l, pages=pages)
            return to_tm(o.reshape(ts, aw)).astype(BF16)

        pool_buf_s = jnp.swapaxes(state_pool[l], 0, 1).reshape(1, POOL_BUF * bs, pw)
        conv_buf_s = jnp.swapaxes(state_ffn_conv[l], 0, 1).reshape(1, CONV_BUF * bs, 2 * f)
        xs, (k, v, _), lf, h_re, h_im, new_pool, new_conv = _layer(
            xs, mod_s, lw, l, attn_s, None,
            state_ssm_re[l].reshape(bs, groups * nstate), state_ssm_im[l].reshape(bs, groups * nstate),
            pool_buf_s, conv_buf_s, nb=bs, seq=ds, time_major=True, pos0=past, dims=dims)
        outs_s.append((from_tm(k).reshape(bs, ds, heads, hd), from_tm(v).reshape(bs, ds, heads, hd),
                       from_tm(lf[:, :heads]),
                       h_re.reshape(bs, groups, nstate), h_im.reshape(bs, groups, nstate),
                       jnp.swapaxes(new_pool.reshape(POOL_BUF, bs, pw), 0, 1),
                       jnp.swapaxes(new_conv.reshape(CONV_BUF, bs, 2 * f), 0, 1)))

    stack = lambda outs: tuple(jnp.stack([o[i] for o in outs]) for i in range(len(outs[0])))
    return ((xp.reshape(bp, seq, d), from_tm(xs)) + kv_p + stack(outs_p) + stack(outs_s))
```

```python
import functools
import math

import jax
import jax.numpy as jnp
from jax import lax
from jax.experimental import pallas as pl
from jax.experimental.pallas import tpu as pltpu

F32 = jnp.float32
BF16 = jnp.bfloat16

EPS = 1e-6
POOL_WINDOWS = (2, 4, 8, 16)
POOL_BUF = max(POOL_WINDOWS) - 1
POOL_HDR = POOL_BUF + 1
CONV_WIDTH = 3
CONV_BUF = CONV_WIDTH - 1
CONV_HDR = 16
SUBLANES = 8
LANES = 128
MIB = 1024 * 1024
NEG_BIG = -1e30
LOG2E = math.log2(math.e)


def _cparams(semantics, vmem_mib):
    return pltpu.CompilerParams(dimension_semantics=semantics, vmem_limit_bytes=vmem_mib * MIB)


def _resident(shape):
    nd = len(shape)
    return pl.BlockSpec(shape, lambda *_: (0,) * nd, pipeline_mode=pl.Buffered(1))


def _layer_resident(arr, layer):
    nd = arr.ndim - 1
    return pl.BlockSpec((None,) + arr.shape[1:], lambda *_: (layer,) + (0,) * nd, pipeline_mode=pl.Buffered(1))


def _dot(a, b):
    return jnp.dot(a, b, preferred_element_type=F32)


def _dot_nt(a, b):
    return lax.dot_general(a, b, (((1,), (1,)), ((), ())), preferred_element_type=F32)


def _rms(x, g):
    return x * lax.rsqrt(jnp.mean(x * x, axis=-1, keepdims=True) + EPS) * g


def _gelu_tanh(x):
    c = math.sqrt(2.0 / math.pi)
    return 0.5 * x * (1.0 + jnp.tanh(c * (x + 0.044715 * (x * x * x))))


def _log_sigmoid(x):
    return jnp.minimum(x, 0.0) - jnp.log1p(jnp.exp(-jnp.abs(x)))


def _div_pow2(x, c):
    assert c > 0 and c & (c - 1) == 0
    return x >> (c.bit_length() - 1)


def _row_cumsum(x):
    n = x.shape[0]
    row = lax.broadcasted_iota(jnp.int32, x.shape, 0)
    k = 1
    while k < n:
        x = x + jnp.where(row >= k, pltpu.roll(x, k, axis=0), 0.0)
        k *= 2
    return x


def _ada_kernel(c_ref, w_ref, b_ref, o_ref):
    o_ref[...] = _dot(c_ref[...], w_ref[...].astype(BF16)) + b_ref[...]


def _ada(c_rows, w_ada, b_ada):
    depth, d, n = w_ada.shape
    r = c_rows.shape[0]
    tn = 1024
    return pl.pallas_call(
        _ada_kernel,
        grid=(depth, n // tn),
        in_specs=[pl.BlockSpec((r, d), lambda l, j: (0, 0)),
                  pl.BlockSpec((None, d, tn), lambda l, j: (l, 0, j)),
                  pl.BlockSpec((None, 1, tn), lambda l, j: (l, 0, j))],
        out_specs=pl.BlockSpec((None, r, tn), lambda l, j: (l, 0, j)),
        out_shape=jax.ShapeDtypeStruct((depth, r, n), F32),
        compiler_params=_cparams(("arbitrary", "arbitrary"), 40),
        name="ada",
    )(c_rows, w_ada, b_ada.reshape(depth, 1, n))


def _repack_kernel(w_ref, qkv_ref, f_ref, rest_ref, *, off_f, heads):
    qkv_ref[...] = w_ref[0:off_f, :].astype(BF16)
    f_ref[...] = w_ref[off_f:off_f + heads, :]
    rest_ref[...] = w_ref[off_f + heads:, :].astype(BF16)


def _repack(w_t, off_f, heads):
    depth, n, d = w_t.shape
    tc = _pick(d, 256)
    nrest = n - off_f - heads
    assert off_f % SUBLANES == 0 and heads % SUBLANES == 0
    return pl.pallas_call(
        functools.partial(_repack_kernel, off_f=off_f, heads=heads),
        grid=(depth, d // tc),
        in_specs=[pl.BlockSpec((None, n, tc), lambda l, i: (l, 0, i))],
        out_specs=[pl.BlockSpec((None, off_f, tc), lambda l, i: (l, 0, i)),
                   pl.BlockSpec((None, heads, tc), lambda l, i: (l, 0, i)),
                   pl.BlockSpec((None, nrest, tc), lambda l, i: (l, 0, i))],
        out_shape=[jax.ShapeDtypeStruct((depth, off_f, d), BF16), jax.ShapeDtypeStruct((depth, heads, d), F32),
                   jax.ShapeDtypeStruct((depth, nrest, d), BF16)],
        compiler_params=_cparams(("arbitrary", "arbitrary"), 48),
        name="repack",
    )(w_t)


def _ssm_disc_kernel(are_ref, aim_ref, ls_ref, abre_ref, abim_ref, zre_ref, zim_ref):
    a_re, a_im = are_ref[...], aim_ref[...]
    step = jnp.exp(ls_ref[...])
    mag = jnp.exp(step * a_re)
    ab_re = mag * jnp.cos(step * a_im)
    ab_im = mag * jnp.sin(step * a_im)
    den = a_re * a_re + a_im * a_im
    abre_ref[...] = ab_re
    abim_ref[...] = ab_im
    zre_ref[...] = ((ab_re - 1.0) * a_re + ab_im * a_im) / den
    zim_ref[...] = (ab_im * a_re - (ab_re - 1.0) * a_im) / den


def _ssm_disc(a_re, a_im, log_step):
    depth, g, p = a_re.shape
    n = g * p
    ls = jnp.broadcast_to(log_step[:, :, None], (depth, g, p)).reshape(depth, n)
    spec = pl.BlockSpec((depth, n), lambda: (0, 0))
    return pl.pallas_call(
        _ssm_disc_kernel,
        in_specs=[spec] * 3,
        out_specs=[spec] * 4,
        out_shape=[jax.ShapeDtypeStruct((depth, n), F32)] * 4,
        name="ssm_disc",
    )(a_re.reshape(depth, n), a_im.reshape(depth, n), ls)


def _inproj_kernel(x_ref, g_ref, sc_ref, sh_ref, wqt_ref, wvt_ref, wk_ref, wsp_ref, wf_ref, bf_ref, *rest,
                   aw, sw, pw, hd, transposed, c2):
    h = _rms(x_ref[...], g_ref[...]) * (1.0 + sc_ref[...]) + sh_ref[...]
    hb = h.astype(BF16)
    k = _dot_nt(hb, wk_ref[...])
    v = _dot_nt(hb, wvt_ref[...])
    if transposed:
        h_ref, q_ref, kb_ref, vt_ref, us_ref, up_ref, lf_ref, k5_ref, v5_ref = rest[-9:]
        q_ref[...] = (_dot_nt(wqt_ref[...], hb) * c2).astype(q_ref.dtype)
        vt_ref[...] = _dot_nt(wvt_ref[...], hb).astype(vt_ref.dtype)
        kb_ref[...] = k.astype(kb_ref.dtype)
        k5_ref[...] = pltpu.einshape("htd->thd", jnp.stack([k[:, hh * hd:(hh + 1) * hd] for hh in range(aw // hd)]))
        v5_ref[...] = pltpu.einshape("htd->thd", jnp.stack([v[:, hh * hd:(hh + 1) * hd] for hh in range(aw // hd)]))
    else:
        h_ref, q_ref, k_ref, v_ref, us_ref, up_ref, lf_ref = rest
        q_ref[...] = _dot_nt(hb, wqt_ref[...])
        k_ref[...] = k
        v_ref[...] = v
    h_ref[...] = hb
    us_ref[...] = _dot_nt(hb, wsp_ref[0:sw, :])
    up_ref[...] = _dot_nt(hb, wsp_ref[sw:sw + pw, :])
    lf_ref[...] = _log_sigmoid(_dot(hb, wf_ref[...]) + bf_ref[...])


def _inproj(x, g, mod, lw, layer, kv_out, *, tm, seq, nb, sw, pw, hd, time_major, c2, depth=None):
    t, d = x.shape
    wqkv, wrest, wf, bf = lw["wqkv"], lw["wrest"], lw["wf"], lw["b_f"]
    aw = wqkv.shape[1] // 3
    heads = aw // hd
    rb = mod.shape[2]
    row = lambda w: pl.BlockSpec((tm, w), lambda i: (i, 0))
    colT = pl.BlockSpec((aw, tm), lambda i: (0, i))
    common_in = [row(d), _layer_resident(lw["g_pre_mix"], layer)]
    once = dict(pipeline_mode=pl.Buffered(1))
    qkv_rows = lambda part: pl.BlockSpec((None, aw, d), lambda *_: (layer, part, 0), **once)
    weights = [qkv_rows(0), qkv_rows(2), qkv_rows(1),
               pl.BlockSpec((None, sw + pw, d), lambda *_: (layer, 0, 0), **once),
               _layer_resident(wf, layer), _layer_resident(bf, layer)]
    sd = jax.ShapeDtypeStruct
    if time_major:
        assert tm == t
        modspec = lambda kk: pl.BlockSpec((None, None, rb, d), lambda i: (kk, 0, 0, 0))
        return pl.pallas_call(
            functools.partial(_inproj_kernel, aw=aw, sw=sw, pw=pw, hd=hd, transposed=False, c2=c2),
            grid=(1,),
            in_specs=common_in + [modspec(1), modspec(0)] + weights,
            out_specs=[row(d), row(aw), row(aw), row(aw), row(sw), row(pw), row(LANES)],
            out_shape=[sd((t, d), BF16), sd((t, aw), F32), sd((t, aw), F32), sd((t, aw), F32),
                       sd((t, sw), F32), sd((t, pw), F32), sd((t, LANES), F32)],
            compiler_params=_cparams(("arbitrary",), 48),
            name="inproj",
        )(x, lw["g_pre_mix"], mod, mod, wqkv, wqkv, wqkv, wrest, wf, bf)
    tpb = seq // tm
    modspec = lambda kk: pl.BlockSpec((None, None, rb, d), lambda i: (kk, i // tpb, 0, 0))
    slab = pl.BlockSpec((None, None, tm, heads, hd), lambda i: (layer, i // tpb, i % tpb, 0, 0))
    kv_shape = (depth, nb, seq, heads, hd)
    carried = [] if kv_out is None else list(kv_out)
    n_in = 2 + 2 + len(weights)
    outs = pl.pallas_call(
        functools.partial(_inproj_kernel, aw=aw, sw=sw, pw=pw, hd=hd, transposed=True, c2=c2),
        grid=(t // tm,),
        in_specs=common_in + [modspec(1), modspec(0)] + weights + [pl.BlockSpec(memory_space=pl.ANY)] * len(carried),
        out_specs=[row(d), colT, row(aw), colT, pl.BlockSpec((tm, sw), lambda i: (i % tpb, i // tpb)),
                   row(pw), row(LANES), slab, slab],
        out_shape=[sd((t, d), BF16), sd((aw, t), BF16), sd((t, aw), BF16), sd((aw, t), BF16),
                   sd((seq, nb * sw), F32), sd((t, pw), F32), sd((t, LANES), F32),
                   sd(kv_shape, F32), sd(kv_shape, F32)],
        input_output_aliases={n_in + c: 7 + c for c in range(len(carried))},
        compiler_params=_cparams(("arbitrary",), 48),
        name="inproj",
    )(x, lw["g_pre_mix"], mod, mod, wqkv, wqkv, wqkv, wrest, wf, bf, *carried)
    return outs[:7], (outs[7], outs[8])


def _gates_kernel(h_ref, w_ref, o_ref):
    o_ref[...] = jax.nn.sigmoid(_dot_nt(h_ref[...], w_ref[...]))


def _gates(h, wrest, layer, *, tm, skip):
    t, d = h.shape
    n = wrest.shape[1] - skip
    tn = 1024
    assert skip % tn == 0
    return pl.pallas_call(
        _gates_kernel,
        grid=(n // tn, t // tm),
        in_specs=[pl.BlockSpec((tm, d), lambda j, i: (i, 0)),
                  pl.BlockSpec((None, tn, d), lambda j, i: (layer, skip // tn + j, 0))],
        out_specs=pl.BlockSpec((tm, tn), lambda j, i: (i, j)),
        out_shape=jax.ShapeDtypeStruct((t, n), F32),
        compiler_params=_cparams(("arbitrary", "arbitrary"), 48),
        name="gates",
    )(h, wrest)


def _fcum_kernel(lf_ref, o_ref, *, heads):
    neg = _row_cumsum(lf_ref[...]) * (-LOG2E)
    hi = neg.astype(BF16).astype(F32)
    r1 = neg - hi
    mid = r1.astype(BF16).astype(F32)
    lo = (r1 - mid).astype(BF16).astype(F32)
    lane = lax.broadcasted_iota(jnp.int32, neg.shape, 1)
    out = jnp.where(lane < heads, hi,
                    jnp.where(lane < 2 * heads, pltpu.roll(mid, heads, axis=1),
                              jnp.where(lane < 3 * heads, pltpu.roll(lo, 2 * heads, axis=1), 0.0)))
    o_ref[...] = out.astype(o_ref.dtype)


def _fcum(lf, seq, heads):
    t, w = lf.shape
    return pl.pallas_call(
        functools.partial(_fcum_kernel, heads=heads),
        grid=(t // seq,),
        in_specs=[pl.BlockSpec((seq, w), lambda b: (b, 0))],
        out_specs=pl.BlockSpec((seq, w), lambda b: (b, 0)),
        out_shape=jax.ShapeDtypeStruct((t, w), BF16),
        name="fcum",
    )(lf)


def _attn_kernel(qt_ref, k_ref, vt_ref, f_ref, o_ref, m_sc, l_sc, acc_sc, *, hd, hpb, heads):
    i, j = pl.program_id(2), pl.program_id(3)
    tq = qt_ref.shape[1]

    @pl.when(j == 0)
    def _():
        m_sc[...] = jnp.full(m_sc.shape, NEG_BIG, F32)
        l_sc[...] = jnp.zeros(l_sc.shape, F32)
        acc_sc[...] = jnp.zeros(acc_sc.shape, F32)

    def block(diagonal):
        for g in range(hpb):
            rows = slice(g * hd, (g + 1) * hd)
            head = pl.program_id(1) * hpb + g
            r = lax.broadcasted_iota(jnp.int32, (LANES, tq), 0)
            pick = jnp.where(r < 3 * heads, jnp.where((r & (heads - 1)) == head, 1.0, 0.0), 0.0).astype(BF16)
            qaug = jnp.concatenate([qt_ref[rows, :], pick], axis=0)
            kaug = jnp.concatenate([k_ref[:, rows], f_ref[...]], axis=1)
            t = _dot(kaug, qaug)
            if diagonal:
                kpos = lax.broadcasted_iota(jnp.int32, t.shape, 0)
                qpos = lax.broadcasted_iota(jnp.int32, t.shape, 1)
                t = jnp.where(kpos <= qpos, t, NEG_BIG)
            m_old = m_sc[g]
            m_new = jnp.maximum(m_old, jnp.max(t, axis=0, keepdims=True))
            alpha = jnp.exp2(m_old - m_new)
            p = jnp.exp2(t - m_new)
            l_sc[g] = alpha * l_sc[g] + jnp.sum(p, axis=0, keepdims=True)
            acc_sc[rows, :] = alpha * acc_sc[rows, :] + _dot(vt_ref[rows, :], p.astype(BF16))
            m_sc[g] = m_new

    @pl.when(j < i)
    def _():
        block(False)

    @pl.when(j == i)
    def _():
        block(True)
        for g in range(hpb):
            rows = slice(g * hd, (g + 1) * hd)
            o_ref[:, rows] = (acc_sc[rows, :] / l_sc[g]).T.astype(o_ref.dtype)


def _attn_prompt(qt, k, vt, faug, *, nb, seq, heads, hd, tq, hpb):
    t = k.shape[0]
    nq = seq // tq
    w = hpb * hd
    assert heads & (heads - 1) == 0 and 3 * heads <= LANES
    kblk = lambda b, i, j: b * nq + jnp.minimum(j, i)
    return pl.pallas_call(
        functools.partial(_attn_kernel, hd=hd, hpb=hpb, heads=heads),
        grid=(nb, heads // hpb, nq, nq),
        in_specs=[pl.BlockSpec((w, tq), lambda b, h, i, j: (h, b * nq + i)),
                  pl.BlockSpec((tq, w), lambda b, h, i, j: (kblk(b, i, j), h)),
                  pl.BlockSpec((w, tq), lambda b, h, i, j: (h, kblk(b, i, j))),
                  pl.BlockSpec((tq, LANES), lambda b, h, i, j: (kblk(b, i, j), 0))],
        out_specs=pl.BlockSpec((tq, w), lambda b, h, i, j: (b * nq + i, h)),
        out_shape=jax.ShapeDtypeStruct((t, heads * hd), BF16),
        scratch_shapes=[pltpu.VMEM((hpb, 1, tq), F32), pltpu.VMEM((hpb, 1, tq), F32),
                        pltpu.VMEM((w, tq), F32)],
        compiler_params=_cparams(("arbitrary",) * 4, 40),
        name="attn_prompt",
    )(qt, k, vt, faug)


def _attn_sample_kernel(pt_ref, q_ref, kn_ref, vn_ref, lfn_ref, *rest, pages, heads, hd, c2):
    k_refs, v_refs, lf_refs = rest[0:pages], rest[pages:2 * pages], rest[2 * pages:3 * pages]
    o_ref, q2_sc, m_sc, l_sc, acc_sc, fc_sc = rest[3 * pages:]
    del pt_ref
    j = pl.program_id(1)
    nq = q_ref.shape[0]
    rows = heads * nq
    psz = k_refs[0].shape[0]
    cols = psz * heads
    assert nq & (nq - 1) == 0 and heads & (heads - 1) == 0

    @pl.when(j == 0)
    def _():
        q2 = jnp.concatenate([q_ref[:, h * hd:(h + 1) * hd] for h in range(heads)], axis=0)
        q2_sc[...] = (q2 * c2).astype(BF16)
        m_sc[...] = jnp.full(m_sc.shape, NEG_BIG, F32)
        l_sc[...] = jnp.zeros(l_sc.shape, F32)
        acc_sc[...] = jnp.zeros(acc_sc.shape, F32)
        fc_sc[...] = jnp.zeros(fc_sc.shape, F32)

    q2 = q2_sc[...]
    row = lax.broadcasted_iota(jnp.int32, (rows, cols), 0)
    col = lax.broadcasted_iota(jnp.int32, (rows, cols), 1)
    own_head = _div_pow2(row, nq) == (col & (heads - 1))

    def head_scan(x, inclusive_prefix):
        lane = lax.broadcasted_iota(jnp.int32, x.shape, 1)
        k = heads
        while k < cols:
            shifted = pltpu.roll(x, k, axis=1)
            x = x + (jnp.where(lane >= k, shifted, 0.0) if inclusive_prefix else shifted)
            k *= 2
        return x

    def attend(pages_kvl, causal=None):
        n = len(pages_kvl)
        lf = jnp.concatenate([x[2] for x in pages_kvl], axis=0) if n > 1 else pages_kvl[0][2]
        cum, tot = head_scan(lf, True), head_scan(lf, False)
        base = fc_sc[...]
        ts = []
        for c, (k3, _, _) in enumerate(pages_kvl):
            f2 = (base + cum[c:c + 1]) * LOG2E
            base = base + tot[c:c + 1]
            t = jnp.where(own_head, _dot_nt(q2, k3.reshape(cols, hd).astype(BF16)) - f2, NEG_BIG)
            if causal is not None:
                t = jnp.where(causal, t, NEG_BIG)
            ts.append(t)
        fc_sc[...] = base
        m_old = m_sc[...]
        m_new = m_old
        for t in ts:
            m_new = jnp.maximum(m_new, jnp.max(t, axis=-1, keepdims=True))
        alpha = jnp.exp2(m_old - m_new)
        l_new = alpha * l_sc[...]
        acc = alpha * acc_sc[...]
        for t, (_, v3, _) in zip(ts, pages_kvl):
            p = jnp.exp2(t - m_new)
            l_new = l_new + jnp.sum(p, axis=-1, keepdims=True)
            acc = acc + _dot(p.astype(BF16), v3.reshape(cols, hd).astype(BF16))
        l_sc[...] = l_new
        acc_sc[...] = acc
        m_sc[...] = m_new

    attend([(k_refs[c][...], v_refs[c][...], lf_refs[c][...]) for c in range(pages)])

    @pl.when(j == pl.num_programs(1) - 1)
    def _():
        zeros = jnp.zeros((psz - nq, heads, hd), F32)
        causal = _div_pow2(col, heads) <= (row & (nq - 1))
        attend([(jnp.concatenate([kn_ref[...], zeros], axis=0), jnp.concatenate([vn_ref[...], zeros], axis=0),
                 lfn_ref[...])], causal)
        o = acc_sc[...] / l_sc[...]
        for h in range(heads):
            o_ref[:, h * hd:(h + 1) * hd] = o[h * nq:(h + 1) * nq].astype(o_ref.dtype)


def _attn_sample(page_table, q, k_new, v_new, lf_new, cache_k, cache_v, cache_lf, layer, *, pages):
    nb, nq, width = q.shape
    _, _, psz, heads, hd = cache_k.shape
    cols = psz * heads
    steps = page_table.shape[1] // pages
    rows = heads * nq
    per_seq = lambda shape: pl.BlockSpec((None,) + shape, lambda b, j, pt: (b,) + (0,) * len(shape))

    def page_spec(shape, c):
        return pl.BlockSpec((None, None) + shape,
                            lambda b, j, pt: (layer, pt[b, j * pages + c]) + (0,) * len(shape))

    in_specs = ([per_seq((nq, width)), per_seq((nq, heads, hd)), per_seq((nq, heads, hd)), per_seq((1, cols))]
                + [page_spec((psz, heads, hd), c) for c in range(pages)]
                + [page_spec((psz, heads, hd), c) for c in range(pages)]
                + [page_spec((1, cols), c) for c in range(pages)])
    return pl.pallas_call(
        functools.partial(_attn_sample_kernel, pages=pages, heads=heads, hd=hd, c2=hd ** -0.5 * LOG2E),
        grid_spec=pltpu.PrefetchScalarGridSpec(
            num_scalar_prefetch=1,
            grid=(nb, steps),
            in_specs=in_specs,
            out_specs=per_seq((nq, width)),
            scratch_shapes=[pltpu.VMEM((rows, hd), BF16), pltpu.VMEM((rows, 1), F32),
                            pltpu.VMEM((rows, 1), F32), pltpu.VMEM((rows, hd), F32),
                            pltpu.VMEM((1, cols), F32)]),
        out_shape=jax.ShapeDtypeStruct((nb, nq, width), F32),
        compiler_params=_cparams(("arbitrary", "arbitrary"), 52),
        name="attn_sample",
    )(page_table, q, k_new, v_new, lf_new, *([cache_k] * pages), *([cache_v] * pages), *([cache_lf] * pages))


def _ssm_kernel(u_ref, bre_ref, bim_ref, cre_ref, cim_ref, d_ref, wglu_ref, abre_ref, abim_ref,
                zre_ref, zim_ref, h0re_ref, h0im_ref, o_ref, hre_ref, him_ref,
                sre, sim, cr, ci, *, nb, lane_chunk):
    c = pl.program_id(0)
    tr, n = sre.shape
    half = SUBLANES // 2

    @pl.when(c == 0)
    def _():
        cr[...] = h0re_ref[...]
        ci[...] = h0im_ref[...]

    u = u_ref[...]
    ub = u.astype(BF16)
    w = u.shape[1]
    nc = LANES * (n // w)
    blocks = [(kb, slice(kb * LANES, (kb + 1) * LANES), slice(kb * nc, (kb + 1) * nc)) for kb in range(w // LANES)]
    for kb, ch, st in blocks:
        bu_re, bu_im = _dot(ub[:, ch], bre_ref[kb]), _dot(ub[:, ch], bim_ref[kb])
        z_re, z_im = zre_ref[:, st], zim_ref[:, st]
        sre[:, st] = z_re * bu_re - z_im * bu_im
        sim[:, st] = z_re * bu_im + z_im * bu_re

    for lc in range(n // lane_chunk):
        lanes = slice(lc * lane_chunk, (lc + 1) * lane_chunk)
        a_re = jnp.broadcast_to(abre_ref[:, lanes], (SUBLANES, lane_chunk))
        a_im = jnp.broadcast_to(abim_ref[:, lanes], (SUBLANES, lane_chunk))
        lower = lax.broadcasted_iota(jnp.int32, (SUBLANES, lane_chunk), 0) < half

        def step(hr, hi, xr, xi):
            return a_re * hr - a_im * hi + xr, a_re * hi + a_im * hr + xi

        def body(r, carry):
            hr, hi = carry
            row = pl.multiple_of(r * SUBLANES, SUBLANES)
            xr, xi = sre[pl.ds(row, SUBLANES), lanes], sim[pl.ds(row, SUBLANES), lanes]
            if nb == SUBLANES:
                nr, ni = step(hr, hi, xr, xi)
            else:
                t1r, t1i = step(pltpu.roll(hr, half, axis=0), pltpu.roll(hi, half, axis=0), xr, xi)
                t2r, t2i = step(pltpu.roll(t1r, half, axis=0), pltpu.roll(t1i, half, axis=0), xr, xi)
                nr, ni = jnp.where(lower, t1r, t2r), jnp.where(lower, t1i, t2i)
            sre[pl.ds(row, SUBLANES), lanes] = nr
            sim[pl.ds(row, SUBLANES), lanes] = ni
            return nr, ni

        hr, hi = lax.fori_loop(0, tr // SUBLANES, body, (cr[:, lanes], ci[:, lanes]))
        cr[:, lanes] = hr
        ci[:, lanes] = hi

    y = jnp.concatenate([_dot(sre[:, st].astype(BF16), cre_ref[kb]) - _dot(sim[:, st].astype(BF16), cim_ref[kb])
                         for kb, _, st in blocks], axis=1) + d_ref[...] * u
    y = _gelu_tanh(y)
    o_ref[...] = (y * jax.nn.sigmoid(_dot(y.astype(BF16), wglu_ref[...]))).astype(o_ref.dtype)

    @pl.when(c == pl.num_programs(0) - 1)
    def _():
        hre_ref[...] = cr[...]
        him_ref[...] = ci[...]


def _ssm(u, lw, layer, h0_re, h0_im, *, nb, tr):
    rows, w = u.shape
    assert w % LANES == 0
    names = ("bre", "bim", "cre", "cim", "ssm_d", "w_glu", "ab_re", "ab_im", "z_re", "z_im")
    n = lw["ab_re"].shape[2]
    assert nb in (SUBLANES // 2, SUBLANES)
    lead = jnp.zeros((SUBLANES - nb, n), F32)
    h0_re, h0_im = jnp.concatenate([lead, h0_re], axis=0), jnp.concatenate([lead, h0_im], axis=0)
    o, h_re, h_im = pl.pallas_call(
        functools.partial(_ssm_kernel, nb=nb, lane_chunk=512),
        grid=(rows // tr,),
        in_specs=[pl.BlockSpec((tr, w), lambda c: (c, 0)),
                  *[_layer_resident(lw[k], layer) for k in names],
                  _resident(h0_re.shape), _resident(h0_im.shape)],
        out_specs=[pl.BlockSpec((tr, w), lambda c: (c, 0)),
                   pl.BlockSpec((SUBLANES, n), lambda c: (0, 0)), pl.BlockSpec((SUBLANES, n), lambda c: (0, 0))],
        out_shape=[jax.ShapeDtypeStruct((rows, w), BF16),
                   jax.ShapeDtypeStruct((SUBLANES, n), F32), jax.ShapeDtypeStruct((SUBLANES, n), F32)],
        scratch_shapes=[pltpu.VMEM((tr, n), F32), pltpu.VMEM((tr, n), F32),
                        pltpu.VMEM((SUBLANES, n), F32), pltpu.VMEM((SUBLANES, n), F32)],
        compiler_params=_cparams(("arbitrary",), 48),
        name="ssm",
    )(u, *[lw[k] for k in names], h0_re, h0_im)
    return o, h_re[SUBLANES - nb:], h_im[SUBLANES - nb:]


def _pool_kernel(u_ref, buf_ref, w_ref, sc_ref, o_ref, nb_ref, ext, *, rs, pos0):
    n = u_ref.shape[0]
    hdr = POOL_HDR * rs
    gw = w_ref.shape[2]
    ext[0:rs, :] = jnp.zeros((rs, ext.shape[1]), F32)
    ext[rs:hdr, :] = buf_ref[...]
    ext[hdr:hdr + n, :] = u_ref[...]
    pos = pos0 + _div_pow2(lax.broadcasted_iota(jnp.int32, (n, gw), 0), rs)
    for gi, win in enumerate(POOL_WINDOWS):
        lanes = slice(gi * gw, (gi + 1) * gw)
        tot = ext[hdr:hdr + n, lanes]
        for back in range(1, win):
            tot = tot + ext[hdr - back * rs:hdr - back * rs + n, lanes]
        cnt = jnp.minimum(pos + 1, win).astype(F32)
        pooled = tot / cnt - u_ref[:, lanes]
        o_ref[:, lanes] = (_dot(pooled.astype(BF16), w_ref[gi]) * sc_ref[:, lanes]).astype(o_ref.dtype)
    nb_ref[...] = ext[n + rs:n + hdr, :]


def _pool(u, buf, w, scale, layer, *, blocks, rs, pos0):
    t, width = u.shape
    n = t // blocks
    hist = POOL_BUF * rs
    return pl.pallas_call(
        functools.partial(_pool_kernel, rs=rs, pos0=pos0),
        grid=(blocks,),
        in_specs=[pl.BlockSpec((n, width), lambda b: (b, 0)),
                  pl.BlockSpec((None, hist, width), lambda b: (b, 0, 0)),
                  _layer_resident(w, layer), _layer_resident(scale, layer)],
        out_specs=[pl.BlockSpec((n, width), lambda b: (b, 0)),
                   pl.BlockSpec((None, hist, width), lambda b: (b, 0, 0))],
        out_shape=[jax.ShapeDtypeStruct((t, width), BF16), jax.ShapeDtypeStruct((blocks, hist, width), F32)],
        scratch_shapes=[pltpu.VMEM((POOL_HDR * rs + n, width), F32)],
        compiler_params=_cparams(("arbitrary",), 48),
        name="pool",
    )(u, buf, w, scale)


def _merge_kernel(gt_ref, oa_ref, os_ref, op_ref, x_ref, gm_ref, gpost_ref, gpre_ref, scf_ref, shf_ref,
                  wb_ref, wo_ref, xo_ref, h2_ref, *, d, aw, sw):
    merged = (gt_ref[:, 0:d] * _dot(oa_ref[...], wb_ref[0:aw, :])
              + gt_ref[:, d:2 * d] * _dot(os_ref[...], wb_ref[aw:aw + sw, :])
              + gt_ref[:, 2 * d:3 * d] * _dot(op_ref[...], wb_ref[aw + sw:, :]))
    y = _dot(merged.astype(BF16), wo_ref[...])
    x = x_ref[...] + gm_ref[...] * _rms(y, gpost_ref[...])
    xo_ref[...] = x
    h2_ref[...] = (_rms(x, gpre_ref[...]) * (1.0 + scf_ref[...]) + shf_ref[...]).astype(h2_ref.dtype)


def _merge(gates, o_att, o_ssm, o_pool, x, mod, g_post, g_pre_ffn, wb, wo, layer, *, tm, seq, time_major,
           aw, sw):
    t, d = x.shape
    rb = mod.shape[2]
    if time_major:
        bidx = lambda i: 0
        os_spec = pl.BlockSpec((tm, sw), lambda i: (i, 0))
    else:
        tpb = seq // tm
        bidx = lambda i: i // tpb
        os_spec = pl.BlockSpec((tm, sw), lambda i: (i % tpb, i // tpb))
    row = lambda w: pl.BlockSpec((tm, w), lambda i: (i, 0))
    modspec = lambda kk: pl.BlockSpec((None, None, rb, d), lambda i: (kk, bidx(i), 0, 0))
    return pl.pallas_call(
        functools.partial(_merge_kernel, d=d, aw=aw, sw=sw),
        grid=(t // tm,),
        in_specs=[row(3 * d), row(aw), os_spec, row(o_pool.shape[1]), row(d),
                  modspec(2), _layer_resident(g_post, layer), _layer_resident(g_pre_ffn, layer),
                  modspec(4), modspec(3), _layer_resident(wb, layer), _layer_resident(wo, layer)],
        out_specs=[row(d), row(d)],
        out_shape=[jax.ShapeDtypeStruct((t, d), F32), jax.ShapeDtypeStruct((t, d), BF16)],
        compiler_params=_cparams(("arbitrary",), 56),
        name="merge",
    )(gates, o_att, o_ssm, o_pool, x, mod, g_post, g_pre_ffn, mod, mod, wb, wo)


def _ffn_up_kernel(h_ref, halo_ref, wg_ref, wv_ref, cwg_ref, cwv_ref, cbg_ref, cbv_ref, bg_ref, bv_ref,
                   act_ref, ncg_ref, ncv_ref, ext, wb, *, rs, tpb):
    i = pl.program_id(1)
    tm = h_ref.shape[0]
    tn = wg_ref.shape[1]
    back = CONV_BUF * rs

    @pl.when(i == 0)
    def _():
        wb[:, 0:tn] = wg_ref[...].astype(BF16)
        wb[:, tn:2 * tn] = wv_ref[...].astype(BF16)

    if rs == 1:
        ext[CONV_HDR - SUBLANES:CONV_HDR, :] = _dot(halo_ref[...], wb[...])

        @pl.when(i % tpb == 0)
        def _():
            ext[CONV_HDR - back:CONV_HDR, 0:tn] = bg_ref[...]
            ext[CONV_HDR - back:CONV_HDR, tn:2 * tn] = bv_ref[...]
    else:
        ext[CONV_HDR - back:CONV_HDR, 0:tn] = bg_ref[...]
        ext[CONV_HDR - back:CONV_HDR, tn:2 * tn] = bv_ref[...]

    cw = jnp.concatenate([cwg_ref[...], cwv_ref[...]], axis=1)
    cb = jnp.concatenate([cbg_ref[...], cbv_ref[...]], axis=1)
    up = _dot(h_ref[...], wb[...])
    ext[CONV_HDR:CONV_HDR + tm, :] = up
    y = cb + cw[CONV_WIDTH - 1:CONV_WIDTH, :] * up
    for tap in range(CONV_WIDTH - 1):
        off = CONV_HDR - (CONV_WIDTH - 1 - tap) * rs
        y = y + cw[tap:tap + 1, :] * ext[off:off + tm, :]
    act_ref[...] = (_gelu_tanh(y[:, 0:tn]) * y[:, tn:2 * tn]).astype(act_ref.dtype)

    ncg_ref[...] = ext[CONV_HDR + tm - back:CONV_HDR + tm, 0:tn]
    ncv_ref[...] = ext[CONV_HDR + tm - back:CONV_HDR + tm, tn:2 * tn]


def _ffn_up(h2, w_up, conv_w, conv_b, layer, buf, *, tm, seq, rs, tn):
    t, d = h2.shape
    f = w_up.shape[2] // 2
    nj = f // tn
    back = CONV_BUF * rs
    nblk = buf.shape[0]
    tpb = max(seq // tm, 1) if rs == 1 else 1
    blk = (lambda i: i // tpb) if rs == 1 else (lambda i: 0)
    halo_rows = tm // SUBLANES
    col = lambda rows, off: pl.BlockSpec((None, rows, tn), lambda j, i: (layer, 0, j + off))
    state = lambda off: pl.BlockSpec((None, back, tn), lambda j, i: (blk(i), 0, j + off))
    return pl.pallas_call(
        functools.partial(_ffn_up_kernel, rs=rs, tpb=tpb),
        grid=(nj, t // tm),
        in_specs=[pl.BlockSpec((tm, d), lambda j, i: (i, 0)),
                  pl.BlockSpec((SUBLANES, d), lambda j, i: (jnp.maximum(i * halo_rows - 1, 0), 0)),
                  col(d, 0), col(d, nj), col(CONV_WIDTH, 0), col(CONV_WIDTH, nj), col(1, 0), col(1, nj),
                  state(0), state(nj)],
        out_specs=[pl.BlockSpec((tm, tn), lambda j, i: (i, j)),
                   pl.BlockSpec((None, back, tn), lambda j, i: (blk(i), 0, j)),
                   pl.BlockSpec((None, back, tn), lambda j, i: (blk(i), 0, j))],
        out_shape=[jax.ShapeDtypeStruct((t, f), BF16),
                   jax.ShapeDtypeStruct((nblk, back, f), F32), jax.ShapeDtypeStruct((nblk, back, f), F32)],
        scratch_shapes=[pltpu.VMEM((CONV_HDR + tm, 2 * tn), F32), pltpu.VMEM((d, 2 * tn), BF16)],
        compiler_params=_cparams(("arbitrary", "arbitrary"), 56),
        name="ffn_up",
    )(h2, h2, w_up, w_up, conv_w, conv_w, conv_b, conv_b, buf, buf)


def _ffn_down_kernel(act_ref, w_ref, x_ref, gf_ref, gpost_ref, o_ref):
    o_ref[...] = x_ref[...] + gf_ref[...] * _rms(_dot(act_ref[...], w_ref[...]), gpost_ref[...])


def _ffn_down(act, w_down, x, mod, g_post, layer, *, tm, seq, time_major):
    t, d = x.shape
    f = act.shape[1]
    rb = mod.shape[2]
    tpb = 1 if time_major else seq // tm
    bidx = (lambda i: 0) if time_major else (lambda i: i // tpb)
    return pl.pallas_call(
        _ffn_down_kernel,
        grid=(t // tm,),
        in_specs=[pl.BlockSpec((tm, f), lambda i: (i, 0)), _layer_resident(w_down, layer),
                  pl.BlockSpec((tm, d), lambda i: (i, 0)),
                  pl.BlockSpec((None, None, rb, d), lambda i: (5, bidx(i), 0, 0)),
                  _layer_resident(g_post, layer)],
        out_specs=pl.BlockSpec((tm, d), lambda i: (i, 0)),
        out_shape=jax.ShapeDtypeStruct((t, d), F32),
        compiler_params=_cparams(("arbitrary",), 56),
        name="ffn_down",
    )(act, w_down, x, mod, g_post)


def _block_diag(w, per):
    depth, g, r, c = w.shape
    eye = jnp.eye(per, dtype=w.dtype)
    w = w.reshape(depth, g // per, per, r, c)
    return (w[:, :, :, :, None, :] * eye[None, None, :, None, :, None]).reshape(depth, g // per, per * r, per * c)


def _pick(n, pref):
    return pref if n % pref == 0 else n


def _layer(x, mod, lw, layer, attn_fn, kv_out, h0_re, h0_im, pool_buf, conv_buf, *, nb, seq, time_major, pos0,
           dims):
    aw, sw, pw = dims["aw"], dims["sw"], dims["pw"]
    t, d = x.shape
    rs = nb if time_major else 1
    tm_in = t if time_major else _pick(seq, 256)
    res = _inproj(x, lw["g_pre_mix"], mod, lw, layer, kv_out, tm=tm_in, seq=seq, nb=nb, sw=sw, pw=pw,
                  hd=dims["hd"], time_major=time_major, c2=dims["c2"], depth=dims["depth"])
    (h, q, k, v, u_ssm, u_pool, lf), kv_out = res if not time_major else (res, None)
    gates = _gates(h, lw["wrest"], layer, tm=t if time_major else _pick(t, 1024), skip=sw + pw)
    o_att = attn_fn(q, k, v, lf)
    o_ssm, h_re, h_im = _ssm(u_ssm.reshape(t, sw), lw, layer, h0_re, h0_im, nb=nb, tr=_pick(t, 512))
    if not time_major:
        o_ssm = o_ssm.reshape(seq, nb * sw)
    o_pool, new_pool = _pool(u_pool, pool_buf, lw["pool_w"], lw["pool_scale"], layer,
                             blocks=1 if time_major else nb, rs=rs, pos0=pos0)
    x, h2 = _merge(gates, o_att, o_ssm, o_pool, x, mod, lw["g_post_mix"], lw["g_pre_ffn"], lw["w_branch"],
                   lw["w_out"], layer, tm=tm_in, seq=seq, time_major=time_major, aw=aw, sw=sw)
    act, ncg, ncv = _ffn_up(h2, lw["w_up"], lw["conv_w"], lw["conv_b"], layer, conv_buf,
                            tm=t if time_major else _pick(seq, 1024), seq=seq, rs=rs, tn=dims["tn_ff"])
    x = _ffn_down(act, lw["w_down"], x, mod, lw["g_post_ffn"], layer, tm=tm_in, seq=seq, time_major=time_major)
    return x, (k, v, kv_out), lf, h_re, h_im, new_pool, jnp.concatenate([ncg, ncv], axis=-1)


def kernel(x_prompt, x_sample, cache_k, cache_v, cache_logf, page_table, state_ssm_re, state_ssm_im, state_pool, state_ffn_conv, c_prompt, c_sample, w_ada, b_ada, g_pre_mix, g_post_mix, g_pre_ffn, g_post_ffn, w_in, b_f, ssm_a_re, ssm_a_im, ssm_log_step, ssm_b_re, ssm_b_im, ssm_c_re, ssm_c_im, ssm_d, w_glu, pool_w, pool_scale, w_branch, w_out, w_up, conv_w, conv_b, w_down):
    bp, seq, d = x_prompt.shape
    bs, ds, _ = x_sample.shape
    depth, n_pool, psz, heads, hd = cache_k.shape
    aw = heads * hd
    groups, nstate = ssm_a_re.shape[1], ssm_a_re.shape[2]
    sw = ssm_d.shape[1]
    pw = pool_scale.shape[1]
    f = w_down.shape[1]
    past = page_table.shape[1] * psz
    tp, ts = bp * seq, bs * ds
    off_f = 3 * aw
    dims = dict(aw=aw, sw=sw, pw=pw, hd=hd, depth=depth, tn_ff=_pick(f, 512), c2=hd ** -0.5 * LOG2E)
    ab_re, ab_im, z_re, z_im = _ssm_disc(ssm_a_re, ssm_a_im, ssm_log_step)
    wqkv, wf_t, wrest = _repack(jnp.swapaxes(w_in, 1, 2), off_f, heads)
    wf = jnp.pad(jnp.swapaxes(wf_t, 1, 2), ((0, 0), (0, 0), (0, LANES - heads))).astype(BF16)
    per = LANES // (sw // groups)
    lw = dict(
        g_pre_mix=g_pre_mix[:, None], g_post_mix=g_post_mix[:, None],
        g_pre_ffn=g_pre_ffn[:, None], g_post_ffn=g_post_ffn[:, None],
        wqkv=wqkv, wrest=wrest, wf=wf,
        b_f=jnp.pad(b_f, ((0, 0), (0, LANES - heads)))[:, None, :],
        bre=_block_diag(jnp.swapaxes(ssm_b_re, 2, 3), per).astype(BF16),
        bim=_block_diag(jnp.swapaxes(ssm_b_im, 2, 3), per).astype(BF16),
        cre=_block_diag(jnp.swapaxes(ssm_c_re, 2, 3), per).astype(BF16),
        cim=_block_diag(jnp.swapaxes(ssm_c_im, 2, 3), per).astype(BF16),
        ssm_d=ssm_d[:, None], w_glu=w_glu.astype(BF16),
        ab_re=ab_re[:, None], ab_im=ab_im[:, None], z_re=z_re[:, None], z_im=z_im[:, None],
        pool_w=pool_w.astype(BF16), pool_scale=pool_scale[:, None],
        w_branch=w_branch.astype(BF16), w_out=w_out.astype(BF16),
        w_up=w_up, conv_w=conv_w, conv_b=conv_b[:, None], w_down=w_down.astype(BF16))

    n_c = bp + bs
    c_rows = jnp.concatenate([c_prompt, c_sample, jnp.zeros((-n_c % 16, d), F32)], axis=0).astype(BF16)
    mod_all = _ada(c_rows, w_ada, b_ada)

    cache_lf = cache_logf.reshape(depth, n_pool, 1, psz * heads)
    pages = next(p for p in (16, 8, 4, 2, 1) if page_table.shape[1] % p == 0)

    def to_tm(a):
        return jnp.swapaxes(a.reshape((bs, ds) + a.shape[1:]), 0, 1).reshape((ts,) + a.shape[1:])

    def from_tm(a):
        return jnp.swapaxes(a.reshape((ds, bs) + a.shape[1:]), 0, 1)

    xp = x_prompt.reshape(tp, d)
    xs = to_tm(x_sample.reshape(ts, d))
    kv_p = None
    outs_p, outs_s = [], []
    for l in range(depth):
        mod_p = jnp.swapaxes(mod_all[l, :bp].reshape(bp, 6, 1, d), 0, 1)

        def attn_p(qt, k, vt, lf):
            return _attn_prompt(qt, k, vt, _fcum(lf, seq, heads), nb=bp, seq=seq, heads=heads, hd=hd,
                                tq=_pick(seq, 512), hpb=4)

        zst = jnp.zeros((bp, groups * nstate), F32)
        xp, (_, _, kv_p), lf, h_re, h_im, new_pool, new_conv = _layer(
            xp, mod_p, lw, l, attn_p, kv_p, zst, zst, jnp.zeros((bp, POOL_BUF, pw), F32),
            jnp.zeros((bp, CONV_BUF, 2 * f), F32),
            nb=bp, seq=seq, time_major=False, pos0=0, dims=dims)
        outs_p.append((lf[:, :heads].reshape(bp, seq, heads),
                       h_re.reshape(bp, groups, nstate), h_im.reshape(bp, groups, nstate),
                       new_pool, new_conv))

        mod_s = jnp.tile(mod_all[l, bp:n_c], (ds, 1)).reshape(ts, 6, d).transpose(1, 0, 2)[:, None]

        def attn_s(q, k, v, lf, l=l):
            lf_new = from_tm(lf[:, :heads]).reshape(bs, 1, ds * heads)
            lf_new = jnp.pad(lf_new, ((0, 0), (0, 0), (0, (psz - ds) * heads)))
            o = _attn_sample(page_table, from_tm(q), from_tm(k).reshape(bs, ds, heads, hd),
                             from_tm(v).reshape(bs, ds, heads, hd), lf_new,
                             cache_k, cache_v, cache_lf, l, pages=pages)
            return to_tm(o.reshape(ts, aw)).astype(BF16)

        pool_buf_s = jnp.swapaxes(state_pool[l], 0, 1).reshape(1, POOL_BUF * bs, pw)
        conv_buf_s = jnp.swapaxes(state_ffn_conv[l], 0, 1).reshape(1, CONV_BUF * bs, 2 * f)
        xs, (k, v, _), lf, h_re, h_im, new_pool, new_conv = _layer(
            xs, mod_s, lw, l, attn_s, None,
            state_ssm_re[l].reshape(bs, groups * nstate), state_ssm_im[l].reshape(bs, groups * nstate),
            pool_buf_s, conv_buf_s, nb=bs, seq=ds, time_major=True, pos0=past, dims=dims)
        outs_s.append((from_tm(k).reshape(bs, ds, heads, hd), from_tm(v).reshape(bs, ds, heads, hd),
                       from_tm(lf[:, :heads]),
                       h_re.reshape(bs, groups, nstate), h_im.reshape(bs, groups, nstate),
                       jnp.swapaxes(new_pool.reshape(POOL_BUF, bs, pw), 0, 1),
                       jnp.swapaxes(new_conv.reshape(CONV_BUF, bs, 2 * f), 0, 1)))

    stack = lambda outs: tuple(jnp.stack([o[i] for o in outs]) for i in range(len(outs[0])))
    return ((xp.reshape(bp, seq, d), from_tm(xs)) + kv_p + stack(outs_p) + stack(outs_s))
```

```python
import functools
import math

import jax
import jax.numpy as jnp
from jax import lax
from jax.experimental import pallas as pl
from jax.experimental.pallas import tpu as pltpu

F32 = jnp.float32
BF16 = jnp.bfloat16

EPS = 1e-6
POOL_WINDOWS = (2, 4, 8, 16)
POOL_BUF = max(POOL_WINDOWS) - 1
POOL_HDR = POOL_BUF + 1
CONV_WIDTH = 3
CONV_BUF = CONV_WIDTH - 1
CONV_HDR = 16
SUBLANES = 8
LANES = 128
MIB = 1024 * 1024
NEG_BIG = -1e30
LOG2E = math.log2(math.e)


def _cparams(semantics, vmem_mib):
    return pltpu.CompilerParams(dimension_semantics=semantics, vmem_limit_bytes=vmem_mib * MIB)


def _resident(shape):
    nd = len(shape)
    return pl.BlockSpec(shape, lambda *_: (0,) * nd, pipeline_mode=pl.Buffered(1))


def _layer_resident(arr, layer):
    nd = arr.ndim - 1
    return pl.BlockSpec((None,) + arr.shape[1:], lambda *_: (layer,) + (0,) * nd, pipeline_mode=pl.Buffered(1))


def _dot(a, b):
    return jnp.dot(a, b, preferred_element_type=F32)


def _dot_nt(a, b):
    return lax.dot_general(a, b, (((1,), (1,)), ((), ())), preferred_element_type=F32)


def _rms(x, g):
    return x * lax.rsqrt(jnp.mean(x * x, axis=-1, keepdims=True) + EPS) * g


def _gelu_tanh(x):
    c = math.sqrt(2.0 / math.pi)
    return 0.5 * x * (1.0 + jnp.tanh(c * (x + 0.044715 * (x * x * x))))


def _log_sigmoid(x):
    return jnp.minimum(x, 0.0) - jnp.log1p(jnp.exp(-jnp.abs(x)))


def _div_pow2(x, c):
    assert c > 0 and c & (c - 1) == 0
    return x >> (c.bit_length() - 1)


def _row_cumsum(x):
    n = x.shape[0]
    row = lax.broadcasted_iota(jnp.int32, x.shape, 0)
    k = 1
    while k < n:
        x = x + jnp.where(row >= k, pltpu.roll(x, k, axis=0), 0.0)
        k *= 2
    return x


def _ada_kernel(c_ref, w_ref, b_ref, o_ref):
    o_ref[...] = _dot(c_ref[...], w_ref[...].astype(BF16)) + b_ref[...]


def _ada(c_rows, w_ada, b_ada):
    depth, d, n = w_ada.shape
    r = c_rows.shape[0]
    tn = 1024
    return pl.pallas_call(
        _ada_kernel,
        grid=(depth, n // tn),
        in_specs=[pl.BlockSpec((r, d), lambda l, j: (0, 0)),
                  pl.BlockSpec((None, d, tn), lambda l, j: (l, 0, j)),
                  pl.BlockSpec((None, 1, tn), lambda l, j: (l, 0, j))],
        out_specs=pl.BlockSpec((None, r, tn), lambda l, j: (l, 0, j)),
        out_shape=jax.ShapeDtypeStruct((depth, r, n), F32),
        compiler_params=_cparams(("arbitrary", "arbitrary"), 40),
        name="ada",
    )(c_rows, w_ada, b_ada.reshape(depth, 1, n))


def _repack_kernel(w_ref, qkv_ref, f_ref, rest_ref, *, off_f, heads):
    qkv_ref[...] = w_ref[0:off_f, :].astype(BF16)
    f_ref[...] = w_ref[off_f:off_f + heads, :]
    rest_ref[...] = w_ref[off_f + heads:, :].astype(BF16)


def _repack(w_t, off_f, heads):
    depth, n, d = w_t.shape
    tc = _pick(d, 256)
    nrest = n - off_f - heads
    assert off_f % SUBLANES == 0 and heads % SUBLANES == 0
    return pl.pallas_call(
        functools.partial(_repack_kernel, off_f=off_f, heads=heads),
        grid=(depth, d // tc),
        in_specs=[pl.BlockSpec((None, n, tc), lambda l, i: (l, 0, i))],
        out_specs=[pl.BlockSpec((None, off_f, tc), lambda l, i: (l, 0, i)),
                   pl.BlockSpec((None, heads, tc), lambda l, i: (l, 0, i)),
                   pl.BlockSpec((None, nrest, tc), lambda l, i: (l, 0, i))],
        out_shape=[jax.ShapeDtypeStruct((depth, off_f, d), BF16), jax.ShapeDtypeStruct((depth, heads, d), F32),
                   jax.ShapeDtypeStruct((depth, nrest, d), BF16)],
        compiler_params=_cparams(("arbitrary", "arbitrary"), 48),
        name="repack",
    )(w_t)


def _ssm_disc_kernel(are_ref, aim_ref, ls_ref, abre_ref, abim_ref, zre_ref, zim_ref):
    a_re, a_im = are_ref[...], aim_ref[...]
    step = jnp.exp(ls_ref[...])
    mag = jnp.exp(step * a_re)
    ab_re = mag * jnp.cos(step * a_im)
    ab_im = mag * jnp.sin(step * a_im)
    den = a_re * a_re + a_im * a_im
    abre_ref[...] = ab_re
    abim_ref[...] = ab_im
    zre_ref[...] = ((ab_re - 1.0) * a_re + ab_im * a_im) / den
    zim_ref[...] = (ab_im * a_re - (ab_re - 1.0) * a_im) / den


def _ssm_disc(a_re, a_im, log_step):
    depth, g, p = a_re.shape
    n = g * p
    ls = jnp.broadcast_to(log_step[:, :, None], (depth, g, p)).reshape(depth, n)
    spec = pl.BlockSpec((depth, n), lambda: (0, 0))
    return pl.pallas_call(
        _ssm_disc_kernel,
        in_specs=[spec] * 3,
        out_specs=[spec] * 4,
        out_shape=[jax.ShapeDtypeStruct((depth, n), F32)] * 4,
        name="ssm_disc",
    )(a_re.reshape(depth, n), a_im.reshape(depth, n), ls)


def _inproj_kernel(x_ref, g_ref, sc_ref, sh_ref, wqt_ref, wvt_ref, wk_ref, wsp_ref, wf_ref, bf_ref, *rest,
                   aw, sw, pw, hd, transposed, c2):
    h = _rms(x_ref[...], g_ref[...]) * (1.0 + sc_ref[...]) + sh_ref[...]
    hb = h.astype(BF16)
    k = _dot_nt(hb, wk_ref[...])
    v = _dot_nt(hb, wvt_ref[...])
    if transposed:
        h_ref, q_ref, kb_ref, vt_ref, us_ref, up_ref, lf_ref, k5_ref, v5_ref = rest[-9:]
        q_ref[...] = (_dot_nt(wqt_ref[...], hb) * c2).astype(q_ref.dtype)
        vt_ref[...] = _dot_nt(wvt_ref[...], hb).astype(vt_ref.dtype)
        kb_ref[...] = k.astype(kb_ref.dtype)
        k5_ref[...] = pltpu.einshape("htd->thd", jnp.stack([k[:, hh * hd:(hh + 1) * hd] for hh in range(aw // hd)]))
        v5_ref[...] = pltpu.einshape("htd->thd", jnp.stack([v[:, hh * hd:(hh + 1) * hd] for hh in range(aw // hd)]))
    else:
        h_ref, q_ref, k_ref, v_ref, us_ref, up_ref, lf_ref = rest
        q_ref[...] = _dot_nt(hb, wqt_ref[...])
        k_ref[...] = k
        v_ref[...] = v
    h_ref[...] = hb
    us_ref[...] = _dot_nt(hb, wsp_ref[0:sw, :])
    up_ref[...] = _dot_nt(hb, wsp_ref[sw:sw + pw, :])
    lf_ref[...] = _log_sigmoid(_dot(hb, wf_ref[...]) + bf_ref[...])


def _inproj(x, mod, lw, layer, kv_out, *, tm, seq, nb, sw, pw, hd, time_major, c2):
    t, d = x.shape
    wqkv, wrest, wf, bf = lw["wqkv"], lw["wrest"], lw["wf"], lw["b_f"]
    aw = wqkv.shape[1] // 3
    heads = aw // hd
    rb = mod.shape[2]
    row = lambda w: pl.BlockSpec((tm, w), lambda i: (i, 0))
    colT = pl.BlockSpec((aw, tm), lambda i: (0, i))
    common_in = [row(d), _layer_resident(lw["g_pre_mix"], layer)]
    once = dict(pipeline_mode=pl.Buffered(1))
    qkv_rows = lambda part: pl.BlockSpec((None, aw, d), lambda *_: (layer, part, 0), **once)
    weights = [qkv_rows(0), qkv_rows(2), qkv_rows(1),
               pl.BlockSpec((None, sw + pw, d), lambda *_: (layer, 0, 0), **once),
               _layer_resident(wf, layer), _layer_resident(bf, layer)]
    sd = jax.ShapeDtypeStruct
    if time_major:
        assert tm == t
        modspec = lambda kk: pl.BlockSpec((None, None, rb, d), lambda i: (kk, 0, 0, 0))
        return pl.pallas_call(
            functools.partial(_inproj_kernel, aw=aw, sw=sw, pw=pw, hd=hd, transposed=False, c2=c2),
            grid=(1,),
            in_specs=common_in + [modspec(1), modspec(0)] + weights,
            out_specs=[row(d), row(aw), row(aw), row(aw), row(sw), row(pw), row(LANES)],
            out_shape=[sd((t, d), BF16), sd((t, aw), F32), sd((t, aw), F32), sd((t, aw), F32),
                       sd((t, sw), F32), sd((t, pw), F32), sd((t, LANES), F32)],
            compiler_params=_cparams(("arbitrary",), 48),
            name="inproj",
        )(x, lw["g_pre_mix"], mod, mod, wqkv, wqkv, wqkv, wrest, wf, bf)
    tpb = seq // tm
    modspec = lambda kk: pl.BlockSpec((None, None, rb, d), lambda i: (kk, i // tpb, 0, 0))
    slab = pl.BlockSpec((None, None, tm, heads, hd), lambda i: (layer, i // tpb, i % tpb, 0, 0))
    kv_shape = (wqkv.shape[0], nb, seq, heads, hd)
    carried = [] if kv_out is None else list(kv_out)
    n_in = 2 + 2 + len(weights)
    outs = pl.pallas_call(
        functools.partial(_inproj_kernel, aw=aw, sw=sw, pw=pw, hd=hd, transposed=True, c2=c2),
        grid=(t // tm,),
        in_specs=common_in + [modspec(1), modspec(0)] + weights + [pl.BlockSpec(memory_space=pl.ANY)] * len(carried),
        out_specs=[row(d), colT, row(aw), colT, pl.BlockSpec((tm, sw), lambda i: (i % tpb, i // tpb)),
                   row(pw), row(LANES), slab, slab],
        out_shape=[sd((t, d), BF16), sd((aw, t), BF16), sd((t, aw), BF16), sd((aw, t), BF16),
                   sd((seq, nb * sw), F32), sd((t, pw), F32), sd((t, LANES), F32),
                   sd(kv_shape, F32), sd(kv_shape, F32)],
        input_output_aliases={n_in + c: 7 + c for c in range(len(carried))},
        compiler_params=_cparams(("arbitrary",), 48),
        name="inproj",
    )(x, lw["g_pre_mix"], mod, mod, wqkv, wqkv, wqkv, wrest, wf, bf, *carried)
    return outs[:7], (outs[7], outs[8])


def _gates_kernel(h_ref, w_ref, o_ref):
    o_ref[...] = jax.nn.sigmoid(_dot_nt(h_ref[...], w_ref[...]))


def _gates(h, wrest, layer, *, tm, skip):
    t, d = h.shape
    n = wrest.shape[1] - skip
    tn = 1024
    assert skip % tn == 0
    return pl.pallas_call(
        _gates_kernel,
        grid=(n // tn, t // tm),
        in_specs=[pl.BlockSpec((tm, d), lambda j, i: (i, 0)),
                  pl.BlockSpec((None, tn, d), lambda j, i: (layer, skip // tn + j, 0))],
        out_specs=pl.BlockSpec((tm, tn), lambda j, i: (i, j)),
        out_shape=jax.ShapeDtypeStruct((t, n), F32),
        compiler_params=_cparams(("arbitrary", "arbitrary"), 48),
        name="gates",
    )(h, wrest)


def _fcum_kernel(lf_ref, o_ref, *, heads):
    neg = _row_cumsum(lf_ref[...]) * (-LOG2E)
    hi = neg.astype(BF16).astype(F32)
    r1 = neg - hi
    mid = r1.astype(BF16).astype(F32)
    lo = (r1 - mid).astype(BF16).astype(F32)
    lane = lax.broadcasted_iota(jnp.int32, neg.shape, 1)
    out = jnp.where(lane < heads, hi,
                    jnp.where(lane < 2 * heads, pltpu.roll(mid, heads, axis=1),
                              jnp.where(lane < 3 * heads, pltpu.roll(lo, 2 * heads, axis=1), 0.0)))
    o_ref[...] = out.astype(o_ref.dtype)


def _fcum(lf, seq, heads):
    t, w = lf.shape
    return pl.pallas_call(
        functools.partial(_fcum_kernel, heads=heads),
        grid=(t // seq,),
        in_specs=[pl.BlockSpec((seq, w), lambda b: (b, 0))],
        out_specs=pl.BlockSpec((seq, w), lambda b: (b, 0)),
        out_shape=jax.ShapeDtypeStruct((t, w), BF16),
        name="fcum",
    )(lf)


def _attn_kernel(qt_ref, k_ref, vt_ref, f_ref, o_ref, m_sc, l_sc, acc_sc, *, hd, hpb, heads):
    i, j = pl.program_id(2), pl.program_id(3)
    tq = qt_ref.shape[1]

    @pl.when(j == 0)
    def _():
        m_sc[...] = jnp.full(m_sc.shape, NEG_BIG, F32)
        l_sc[...] = jnp.zeros(l_sc.shape, F32)
        acc_sc[...] = jnp.zeros(acc_sc.shape, F32)

    def block(diagonal):
        for g in range(hpb):
            rows = slice(g * hd, (g + 1) * hd)
            head = pl.program_id(1) * hpb + g
            r = lax.broadcasted_iota(jnp.int32, (LANES, tq), 0)
            pick = jnp.where(r < 3 * heads, jnp.where((r & (heads - 1)) == head, 1.0, 0.0), 0.0).astype(BF16)
            qaug = jnp.concatenate([qt_ref[rows, :], pick], axis=0)
            kaug = jnp.concatenate([k_ref[:, rows], f_ref[...]], axis=1)
            t = _dot(kaug, qaug)
            if diagonal:
                kpos = lax.broadcasted_iota(jnp.int32, t.shape, 0)
                qpos = lax.broadcasted_iota(jnp.int32, t.shape, 1)
                t = jnp.where(kpos <= qpos, t, NEG_BIG)
            m_old = m_sc[g]
            m_new = jnp.maximum(m_old, jnp.max(t, axis=0, keepdims=True))
            alpha = jnp.exp2(m_old - m_new)
            p = jnp.exp2(t - m_new)
            l_sc[g] = alpha * l_sc[g] + jnp.sum(p, axis=0, keepdims=True)
            acc_sc[rows, :] = alpha * acc_sc[rows, :] + _dot(vt_ref[rows, :], p.astype(BF16))
            m_sc[g] = m_new

    @pl.when(j < i)
    def _():
        block(False)

    @pl.when(j == i)
    def _():
        block(True)
        for g in range(hpb):
            rows = slice(g * hd, (g + 1) * hd)
            o_ref[:, rows] = (acc_sc[rows, :] / l_sc[g]).T.astype(o_ref.dtype)


def _attn_prompt(qt, k, vt, faug, *, nb, seq, heads, hd, tq, hpb):
    t = k.shape[0]
    nq = seq // tq
    w = hpb * hd
    assert heads & (heads - 1) == 0 and 3 * heads <= LANES
    kblk = lambda b, i, j: b * nq + jnp.minimum(j, i)
    return pl.pallas_call(
        functools.partial(_attn_kernel, hd=hd, hpb=hpb, heads=heads),
        grid=(nb, heads // hpb, nq, nq),
        in_specs=[pl.BlockSpec((w, tq), lambda b, h, i, j: (h, b * nq + i)),
                  pl.BlockSpec((tq, w), lambda b, h, i, j: (kblk(b, i, j), h)),
                  pl.BlockSpec((w, tq), lambda b, h, i, j: (h, kblk(b, i, j))),
                  pl.BlockSpec((tq, LANES), lambda b, h, i, j: (kblk(b, i, j), 0))],
        out_specs=pl.BlockSpec((tq, w), lambda b, h, i, j: (b * nq + i, h)),
        out_shape=jax.ShapeDtypeStruct((t, heads * hd), BF16),
        scratch_shapes=[pltpu.VMEM((hpb, 1, tq), F32), pltpu.VMEM((hpb, 1, tq), F32),
                        pltpu.VMEM((w, tq), F32)],
        compiler_params=_cparams(("arbitrary",) * 4, 40),
        name="attn_prompt",
    )(qt, k, vt, faug)


def _attn_sample_kernel(pt_ref, q_ref, kn_ref, vn_ref, lfn_ref, *rest, pages, heads, hd, c2):
    k_refs, v_refs, lf_refs = rest[0:pages], rest[pages:2 * pages], rest[2 * pages:3 * pages]
    o_ref, q2_sc, m_sc, l_sc, acc_sc, fc_sc = rest[3 * pages:]
    del pt_ref
    j = pl.program_id(1)
    nq = q_ref.shape[0]
    rows = heads * nq
    psz = k_refs[0].shape[0]
    cols = psz * heads
    assert nq & (nq - 1) == 0 and heads & (heads - 1) == 0

    @pl.when(j == 0)
    def _():
        q2 = jnp.concatenate([q_ref[:, h * hd:(h + 1) * hd] for h in range(heads)], axis=0)
        q2_sc[...] = (q2 * c2).astype(BF16)
        m_sc[...] = jnp.full(m_sc.shape, NEG_BIG, F32)
        l_sc[...] = jnp.zeros(l_sc.shape, F32)
        acc_sc[...] = jnp.zeros(acc_sc.shape, F32)
        fc_sc[...] = jnp.zeros(fc_sc.shape, F32)

    q2 = q2_sc[...]
    row = lax.broadcasted_iota(jnp.int32, (rows, cols), 0)
    col = lax.broadcasted_iota(jnp.int32, (rows, cols), 1)
    own_head = _div_pow2(row, nq) == (col & (heads - 1))

    def head_scan(x, inclusive_prefix):
        lane = lax.broadcasted_iota(jnp.int32, x.shape, 1)
        k = heads
        while k < cols:
            shifted = pltpu.roll(x, k, axis=1)
            x = x + (jnp.where(lane >= k, shifted, 0.0) if inclusive_prefix else shifted)
            k *= 2
        return x

    def attend(pages_kvl, causal=None):
        n = len(pages_kvl)
        lf = jnp.concatenate([x[2] for x in pages_kvl], axis=0) if n > 1 else pages_kvl[0][2]
        cum, tot = head_scan(lf, True), head_scan(lf, False)
        base = fc_sc[...]
        ts = []
        for c, (k3, _, _) in enumerate(pages_kvl):
            f2 = (base + cum[c:c + 1]) * LOG2E
            base = base + tot[c:c + 1]
            t = jnp.where(own_head, _dot_nt(q2, k3.reshape(cols, hd).astype(BF16)) - f2, NEG_BIG)
            if causal is not None:
                t = jnp.where(causal, t, NEG_BIG)
            ts.append(t)
        fc_sc[...] = base
        m_old = m_sc[...]
        m_new = m_old
        for t in ts:
            m_new = jnp.maximum(m_new, jnp.max(t, axis=-1, keepdims=True))
        alpha = jnp.exp2(m_old - m_new)
        l_new = alpha * l_sc[...]
        acc = alpha * acc_sc[...]
        for t, (_, v3, _) in zip(ts, pages_kvl):
            p = jnp.exp2(t - m_new)
            l_new = l_new + jnp.sum(p, axis=-1, keepdims=True)
            acc = acc + _dot(p.astype(BF16), v3.reshape(cols, hd).astype(BF16))
        l_sc[...] = l_new
        acc_sc[...] = acc
        m_sc[...] = m_new

    attend([(k_refs[c][...], v_refs[c][...], lf_refs[c][...]) for c in range(pages)])

    @pl.when(j == pl.num_programs(1) - 1)
    def _():
        zeros = jnp.zeros((psz - nq, heads, hd), F32)
        causal = _div_pow2(col, heads) <= (row & (nq - 1))
        attend([(jnp.concatenate([kn_ref[...], zeros], axis=0), jnp.concatenate([vn_ref[...], zeros], axis=0),
                 lfn_ref[...])], causal)
        o = acc_sc[...] / l_sc[...]
        for h in range(heads):
            o_ref[:, h * hd:(h + 1) * hd] = o[h * nq:(h + 1) * nq].astype(o_ref.dtype)


def _attn_sample(page_table, q, k_new, v_new, lf_new, cache_k, cache_v, cache_lf, layer, *, pages):
    nb, nq, width = q.shape
    _, _, psz, heads, hd = cache_k.shape
    cols = psz * heads
    steps = page_table.shape[1] // pages
    rows = heads * nq
    per_seq = lambda shape: pl.BlockSpec((None,) + shape, lambda b, j, pt: (b,) + (0,) * len(shape))

    def page_spec(shape, c):
        return pl.BlockSpec((None, None) + shape,
                            lambda b, j, pt: (layer, pt[b, j * pages + c]) + (0,) * len(shape))

    in_specs = ([per_seq((nq, width)), per_seq((nq, heads, hd)), per_seq((nq, heads, hd)), per_seq((1, cols))]
                + [page_spec((psz, heads, hd), c) for c in range(pages)]
                + [page_spec((psz, heads, hd), c) for c in range(pages)]
                + [page_spec((1, cols), c) for c in range(pages)])
    return pl.pallas_call(
        functools.partial(_attn_sample_kernel, pages=pages, heads=heads, hd=hd, c2=hd ** -0.5 * LOG2E),
        grid_spec=pltpu.PrefetchScalarGridSpec(
            num_scalar_prefetch=1,
            grid=(nb, steps),
            in_specs=in_specs,
            out_specs=per_seq((nq, width)),
            scratch_shapes=[pltpu.VMEM((rows, hd), BF16), pltpu.VMEM((rows, 1), F32),
                            pltpu.VMEM((rows, 1), F32), pltpu.VMEM((rows, hd), F32),
                            pltpu.VMEM((1, cols), F32)]),
        out_shape=jax.ShapeDtypeStruct((nb, nq, width), F32),
        compiler_params=_cparams(("arbitrary", "arbitrary"), 52),
        name="attn_sample",
    )(page_table, q, k_new, v_new, lf_new, *([cache_k] * pages), *([cache_v] * pages), *([cache_lf] * pages))


def _ssm_kernel(u_ref, bre_ref, bim_ref, cre_ref, cim_ref, d_ref, wglu_ref, abre_ref, abim_ref,
                zre_ref, zim_ref, h0re_ref, h0im_ref, o_ref, hre_ref, him_ref,
                sre, sim, cr, ci, *, nb, lane_chunk, fold):
    c = pl.program_id(0)
    tr, n = sre.shape
    half = SUBLANES // 2

    @pl.when(c == 0)
    def _():
        cr[...] = h0re_ref[...]
        ci[...] = h0im_ref[...]

    u = u_ref[...]
    if fold > 1:
        u = pltpu.einshape("l(bc)->(lb)c", u, b=fold)
    ub = u.astype(BF16)
    w = u.shape[1]
    nc = LANES * (n // w)
    blocks = [(kb, slice(kb * LANES, (kb + 1) * LANES), slice(kb * nc, (kb + 1) * nc)) for kb in range(w // LANES)]
    for kb, ch, st in blocks:
        bu_re, bu_im = _dot(ub[:, ch], bre_ref[kb]), _dot(ub[:, ch], bim_ref[kb])
        z_re, z_im = zre_ref[:, st], zim_ref[:, st]
        sre[:, st] = z_re * bu_re - z_im * bu_im
        sim[:, st] = z_re * bu_im + z_im * bu_re

    for lc in range(n // lane_chunk):
        lanes = slice(lc * lane_chunk, (lc + 1) * lane_chunk)
        a_re = jnp.broadcast_to(abre_ref[:, lanes], (SUBLANES, lane_chunk))
        a_im = jnp.broadcast_to(abim_ref[:, lanes], (SUBLANES, lane_chunk))
        lower = lax.broadcasted_iota(jnp.int32, (SUBLANES, lane_chunk), 0) < half

        def step(hr, hi, xr, xi):
            return a_re * hr - a_im * hi + xr, a_re * hi + a_im * hr + xi

        def body(r, carry):
            hr, hi = carry
            row = pl.multiple_of(r * SUBLANES, SUBLANES)
            xr, xi = sre[pl.ds(row, SUBLANES), lanes], sim[pl.ds(row, SUBLANES), lanes]
            if nb == SUBLANES:
                nr, ni = step(hr, hi, xr, xi)
            else:
                t1r, t1i = step(pltpu.roll(hr, half, axis=0), pltpu.roll(hi, half, axis=0), xr, xi)
                t2r, t2i = step(pltpu.roll(t1r, half, axis=0), pltpu.roll(t1i, half, axis=0), xr, xi)
                nr, ni = jnp.where(lower, t1r, t2r), jnp.where(lower, t1i, t2i)
            sre[pl.ds(row, SUBLANES), lanes] = nr
            sim[pl.ds(row, SUBLANES), lanes] = ni
            return nr, ni

        hr, hi = lax.fori_loop(0, tr // SUBLANES, body, (cr[:, lanes], ci[:, lanes]))
        cr[:, lanes] = hr
        ci[:, lanes] = hi

    y = jnp.concatenate([_dot(sre[:, st].astype(BF16), cre_ref[kb]) - _dot(sim[:, st].astype(BF16), cim_ref[kb])
                         for kb, _, st in blocks], axis=1) + d_ref[...] * u
    y = _gelu_tanh(y)
    o = y * jax.nn.sigmoid(_dot(y.astype(BF16), wglu_ref[...]))
    if fold > 1:
        o = pltpu.einshape("(lb)c->l(bc)", o, b=fold)
    o_ref[...] = o.astype(o_ref.dtype)

    @pl.when(c == pl.num_programs(0) - 1)
    def _():
        hre_ref[...] = cr[...]
        him_ref[...] = ci[...]


def _ssm(u, lw, layer, h0_re, h0_im, *, nb, tr, fold):
    rows, w = u.shape[0] * fold, u.shape[1] // fold
    assert w % LANES == 0 and fold in (1, nb)
    names = ("bre", "bim", "cre", "cim", "ssm_d", "w_glu", "ab_re", "ab_im", "z_re", "z_im")
    n = lw["ab_re"].shape[2]
    assert nb in (SUBLANES // 2, SUBLANES)
    lead = jnp.zeros((SUBLANES - nb, n), F32)
    h0_re, h0_im = jnp.concatenate([lead, h0_re], axis=0), jnp.concatenate([lead, h0_im], axis=0)
    o, h_re, h_im = pl.pallas_call(
        functools.partial(_ssm_kernel, nb=nb, lane_chunk=512, fold=fold),
        grid=(rows // tr,),
        in_specs=[pl.BlockSpec((tr // fold, w * fold), lambda c: (c, 0)),
                  *[_layer_resident(lw[k], layer) for k in names],
                  _resident(h0_re.shape), _resident(h0_im.shape)],
        out_specs=[pl.BlockSpec((tr // fold, w * fold), lambda c: (c, 0)),
                   pl.BlockSpec((SUBLANES, n), lambda c: (0, 0)), pl.BlockSpec((SUBLANES, n), lambda c: (0, 0))],
        out_shape=[jax.ShapeDtypeStruct(u.shape, BF16),
                   jax.ShapeDtypeStruct((SUBLANES, n), F32), jax.ShapeDtypeStruct((SUBLANES, n), F32)],
        scratch_shapes=[pltpu.VMEM((tr, n), F32), pltpu.VMEM((tr, n), F32),
                        pltpu.VMEM((SUBLANES, n), F32), pltpu.VMEM((SUBLANES, n), F32)],
        compiler_params=_cparams(("arbitrary",), 48),
        name="ssm",
    )(u, *[lw[k] for k in names], h0_re, h0_im)
    return o, h_re[SUBLANES - nb:], h_im[SUBLANES - nb:]


def _pool_kernel(u_ref, buf_ref, w_ref, sc_ref, o_ref, nb_ref, ext, *, rs, pos0):
    n = u_ref.shape[0]
    hdr = POOL_HDR * rs
    gw = w_ref.shape[2]
    ext[0:rs, :] = jnp.zeros((rs, ext.shape[1]), F32)
    ext[rs:hdr, :] = buf_ref[...]
    ext[hdr:hdr + n, :] = u_ref[...]
    pos = pos0 + _div_pow2(lax.broadcasted_iota(jnp.int32, (n, gw), 0), rs)
    for gi, win in enumerate(POOL_WINDOWS):
        lanes = slice(gi * gw, (gi + 1) * gw)
        tot = ext[hdr:hdr + n, lanes]
        for back in range(1, win):
            tot = tot + ext[hdr - back * rs:hdr - back * rs + n, lanes]
        cnt = jnp.minimum(pos + 1, win).astype(F32)
        pooled = tot / cnt - u_ref[:, lanes]
        o_ref[:, lanes] = (_dot(pooled.astype(BF16), w_ref[gi]) * sc_ref[:, lanes]).astype(o_ref.dtype)
    nb_ref[...] = ext[n + rs:n + hdr, :]


def _pool(u, buf, w, scale, layer, *, blocks, rs, pos0):
    t, width = u.shape
    n = t // blocks
    hist = POOL_BUF * rs
    return pl.pallas_call(
        functools.partial(_pool_kernel, rs=rs, pos0=pos0),
        grid=(blocks,),
        in_specs=[pl.BlockSpec((n, width), lambda b: (b, 0)),
                  pl.BlockSpec((None, hist, width), lambda b: (b, 0, 0)),
                  _layer_resident(w, layer), _layer_resident(scale, layer)],
        out_specs=[pl.BlockSpec((n, width), lambda b: (b, 0)),
                   pl.BlockSpec((None, hist, width), lambda b: (b, 0, 0))],
        out_shape=[jax.ShapeDtypeStruct((t, width), BF16), jax.ShapeDtypeStruct((blocks, hist, width), F32)],
        scratch_shapes=[pltpu.VMEM((POOL_HDR * rs + n, width), F32)],
        compiler_params=_cparams(("arbitrary",), 48),
        name="pool",
    )(u, buf, w, scale)


def _merge_kernel(gt_ref, oa_ref, os_ref, op_ref, x_ref, gm_ref, gpost_ref, gpre_ref, scf_ref, shf_ref,
                  wb_ref, wo_ref, xo_ref, h2_ref, *, d, aw, sw):
    merged = (gt_ref[:, 0:d] * _dot(oa_ref[...], wb_ref[0:aw, :])
              + gt_ref[:, d:2 * d] * _dot(os_ref[...], wb_ref[aw:aw + sw, :])
              + gt_ref[:, 2 * d:3 * d] * _dot(op_ref[...], wb_ref[aw + sw:, :]))
    y = _dot(merged.astype(BF16), wo_ref[...])
    x = x_ref[...] + gm_ref[...] * _rms(y, gpost_ref[...])
    xo_ref[...] = x
    h2_ref[...] = (_rms(x, gpre_ref[...]) * (1.0 + scf_ref[...]) + shf_ref[...]).astype(h2_ref.dtype)


def _merge(gates, o_att, o_ssm, o_pool, x, mod, g_post, g_pre_ffn, wb, wo, layer, *, tm, seq, time_major,
           aw, sw):
    t, d = x.shape
    rb = mod.shape[2]
    if time_major:
        bidx = lambda i: 0
        os_spec = pl.BlockSpec((tm, sw), lambda i: (i, 0))
    else:
        tpb = seq // tm
        bidx = lambda i: i // tpb
        os_spec = pl.BlockSpec((tm, sw), lambda i: (i % tpb, i // tpb))
    row = lambda w: pl.BlockSpec((tm, w), lambda i: (i, 0))
    modspec = lambda kk: pl.BlockSpec((None, None, rb, d), lambda i: (kk, bidx(i), 0, 0))
    return pl.pallas_call(
        functools.partial(_merge_kernel, d=d, aw=aw, sw=sw),
        grid=(t // tm,),
        in_specs=[row(3 * d), row(aw), os_spec, row(o_pool.shape[1]), row(d),
                  modspec(2), _layer_resident(g_post, layer), _layer_resident(g_pre_ffn, layer),
                  modspec(4), modspec(3), _layer_resident(wb, layer), _layer_resident(wo, layer)],
        out_specs=[row(d), row(d)],
        out_shape=[jax.ShapeDtypeStruct((t, d), F32), jax.ShapeDtypeStruct((t, d), BF16)],
        compiler_params=_cparams(("arbitrary",), 56),
        name="merge",
    )(gates, o_att, o_ssm, o_pool, x, mod, g_post, g_pre_ffn, mod, mod, wb, wo)


def _ffn_up_kernel(h_ref, halo_ref, wg_ref, wv_ref, cwg_ref, cwv_ref, cbg_ref, cbv_ref, bg_ref, bv_ref,
                   act_ref, ncg_ref, ncv_ref, ext, wb, *, rs, tpb):
    i = pl.program_id(1)
    tm = h_ref.shape[0]
    tn = wg_ref.shape[1]
    back = CONV_BUF * rs

    @pl.when(i == 0)
    def _():
        wb[:, 0:tn] = wg_ref[...].astype(BF16)
        wb[:, tn:2 * tn] = wv_ref[...].astype(BF16)

    if rs == 1:
        ext[CONV_HDR - SUBLANES:CONV_HDR, :] = _dot(halo_ref[...], wb[...])

        @pl.when(i % tpb == 0)
        def _():
            ext[CONV_HDR - back:CONV_HDR, 0:tn] = bg_ref[...]
            ext[CONV_HDR - back:CONV_HDR, tn:2 * tn] = bv_ref[...]
    else:
        ext[CONV_HDR - back:CONV_HDR, 0:tn] = bg_ref[...]
        ext[CONV_HDR - back:CONV_HDR, tn:2 * tn] = bv_ref[...]

    cw = jnp.concatenate([cwg_ref[...], cwv_ref[...]], axis=1)
    cb = jnp.concatenate([cbg_ref[...], cbv_ref[...]], axis=1)
    up = _dot(h_ref[...], wb[...])
    ext[CONV_HDR:CONV_HDR + tm, :] = up
    y = cb + cw[CONV_WIDTH - 1:CONV_WIDTH, :] * up
    for tap in range(CONV_WIDTH - 1):
        off = CONV_HDR - (CONV_WIDTH - 1 - tap) * rs
        y = y + cw[tap:tap + 1, :] * ext[off:off + tm, :]
    act_ref[...] = (_gelu_tanh(y[:, 0:tn]) * y[:, tn:2 * tn]).astype(act_ref.dtype)

    ncg_ref[...] = ext[CONV_HDR + tm - back:CONV_HDR + tm, 0:tn]
    ncv_ref[...] = ext[CONV_HDR + tm - back:CONV_HDR + tm, tn:2 * tn]


def _ffn_up(h2, w_up, conv_w, conv_b, layer, buf, *, tm, seq, rs, tn):
    t, d = h2.shape
    f = w_up.shape[2] // 2
    nj = f // tn
    back = CONV_BUF * rs
    nblk = buf.shape[0]
    tpb = max(seq // tm, 1) if rs == 1 else 1
    blk = (lambda i: i // tpb) if rs == 1 else (lambda i: 0)
    halo_rows = tm // SUBLANES
    col = lambda rows, off: pl.BlockSpec((None, rows, tn), lambda j, i: (layer, 0, j + off))
    state = lambda off: pl.BlockSpec((None, back, tn), lambda j, i: (blk(i), 0, j + off))
    return pl.pallas_call(
        functools.partial(_ffn_up_kernel, rs=rs, tpb=tpb),
        grid=(nj, t // tm),
        in_specs=[pl.BlockSpec((tm, d), lambda j, i: (i, 0)),
                  pl.BlockSpec((SUBLANES, d), lambda j, i: (jnp.maximum(i * halo_rows - 1, 0), 0)),
                  col(d, 0), col(d, nj), col(CONV_WIDTH, 0), col(CONV_WIDTH, nj), col(1, 0), col(1, nj),
                  state(0), state(nj)],
        out_specs=[pl.BlockSpec((tm, tn), lambda j, i: (i, j)),
                   pl.BlockSpec((None, back, tn), lambda j, i: (blk(i), 0, j)),
                   pl.BlockSpec((None, back, tn), lambda j, i: (blk(i), 0, j))],
        out_shape=[jax.ShapeDtypeStruct((t, f), BF16),
                   jax.ShapeDtypeStruct((nblk, back, f), F32), jax.ShapeDtypeStruct((nblk, back, f), F32)],
        scratch_shapes=[pltpu.VMEM((CONV_HDR + tm, 2 * tn), F32), pltpu.VMEM((d, 2 * tn), BF16)],
        compiler_params=_cparams(("arbitrary", "arbitrary"), 56),
        name="ffn_up",
    )(h2, h2, w_up, w_up, conv_w, conv_w, conv_b, conv_b, buf, buf)


def _ffn_down_kernel(act_ref, w_ref, x_ref, gf_ref, gpost_ref, o_ref):
    o_ref[...] = x_ref[...] + gf_ref[...] * _rms(_dot(act_ref[...], w_ref[...]), gpost_ref[...])


def _ffn_down(act, w_down, x, mod, g_post, layer, *, tm, seq, time_major):
    t, d = x.shape
    f = act.shape[1]
    rb = mod.shape[2]
    tpb = 1 if time_major else seq // tm
    bidx = (lambda i: 0) if time_major else (lambda i: i // tpb)
    return pl.pallas_call(
        _ffn_down_kernel,
        grid=(t // tm,),
        in_specs=[pl.BlockSpec((tm, f), lambda i: (i, 0)), _layer_resident(w_down, layer),
                  pl.BlockSpec((tm, d), lambda i: (i, 0)),
                  pl.BlockSpec((None, None, rb, d), lambda i: (5, bidx(i), 0, 0)),
                  _layer_resident(g_post, layer)],
        out_specs=pl.BlockSpec((tm, d), lambda i: (i, 0)),
        out_shape=jax.ShapeDtypeStruct((t, d), F32),
        compiler_params=_cparams(("arbitrary",), 56),
        name="ffn_down",
    )(act, w_down, x, mod, g_post)


def _block_diag(w, per):
    depth, g, r, c = w.shape
    eye = jnp.eye(per, dtype=w.dtype)
    w = w.reshape(depth, g // per, per, r, c)
    return (w[:, :, :, :, None, :] * eye[None, None, :, None, :, None]).reshape(depth, g // per, per * r, per * c)


def _pick(n, pref):
    return pref if n % pref == 0 else n


def _layer(x, mod, lw, layer, attn_fn, kv_out, h0_re, h0_im, pool_buf, conv_buf, *, nb, seq, time_major, pos0,
           dims):
    aw, sw, pw = dims["aw"], dims["sw"], dims["pw"]
    t, d = x.shape
    rs = nb if time_major else 1
    tm_in = t if time_major else _pick(seq, 256)
    res = _inproj(x, mod, lw, layer, kv_out, tm=tm_in, seq=seq, nb=nb, sw=sw, pw=pw,
                  hd=dims["hd"], time_major=time_major, c2=dims["c2"])
    (h, q, k, v, u_ssm, u_pool, lf), kv_out = res if not time_major else (res, None)
    gates = _gates(h, lw["wrest"], layer, tm=t if time_major else _pick(t, 1024), skip=sw + pw)
    o_att = attn_fn(q, k, v, lf)
    o_ssm, h_re, h_im = _ssm(u_ssm, lw, layer, h0_re, h0_im, nb=nb, tr=_pick(t, 512),
                             fold=1 if time_major else nb)
    o_pool, new_pool = _pool(u_pool, pool_buf, lw["pool_w"], lw["pool_scale"], layer,
                             blocks=1 if time_major else nb, rs=rs, pos0=pos0)
    x, h2 = _merge(gates, o_att, o_ssm, o_pool, x, mod, lw["g_post_mix"], lw["g_pre_ffn"], lw["w_branch"],
                   lw["w_out"], layer, tm=tm_in, seq=seq, time_major=time_major, aw=aw, sw=sw)
    act, ncg, ncv = _ffn_up(h2, lw["w_up"], lw["conv_w"], lw["conv_b"], layer, conv_buf,
                            tm=t if time_major else _pick(seq, 1024), seq=seq, rs=rs, tn=dims["tn_ff"])
    x = _ffn_down(act, lw["w_down"], x, mod, lw["g_post_ffn"], layer, tm=tm_in, seq=seq, time_major=time_major)
    return x, (k, v, kv_out), lf, h_re, h_im, new_pool, jnp.concatenate([ncg, ncv], axis=-1)


def kernel(x_prompt, x_sample, cache_k, cache_v, cache_logf, page_table, state_ssm_re, state_ssm_im, state_pool, state_ffn_conv, c_prompt, c_sample, w_ada, b_ada, g_pre_mix, g_post_mix, g_pre_ffn, g_post_ffn, w_in, b_f, ssm_a_re, ssm_a_im, ssm_log_step, ssm_b_re, ssm_b_im, ssm_c_re, ssm_c_im, ssm_d, w_glu, pool_w, pool_scale, w_branch, w_out, w_up, conv_w, conv_b, w_down):
    bp, seq, d = x_prompt.shape
    bs, ds, _ = x_sample.shape
    depth, n_pool, psz, heads, hd = cache_k.shape
    aw = heads * hd
    groups, nstate = ssm_a_re.shape[1], ssm_a_re.shape[2]
    sw = ssm_d.shape[1]
    pw = pool_scale.shape[1]
    f = w_down.shape[1]
    past = page_table.shape[1] * psz
    tp, ts = bp * seq, bs * ds
    off_f = 3 * aw
    dims = dict(aw=aw, sw=sw, pw=pw, hd=hd, tn_ff=_pick(f, 512), c2=hd ** -0.5 * LOG2E)
    ab_re, ab_im, z_re, z_im = _ssm_disc(ssm_a_re, ssm_a_im, ssm_log_step)
    wqkv, wf_t, wrest = _repack(jnp.swapaxes(w_in, 1, 2), off_f, heads)
    wf = jnp.pad(jnp.swapaxes(wf_t, 1, 2), ((0, 0), (0, 0), (0, LANES - heads))).astype(BF16)
    per = LANES // (sw // groups)
    lw = dict(
        g_pre_mix=g_pre_mix[:, None], g_post_mix=g_post_mix[:, None],
        g_pre_ffn=g_pre_ffn[:, None], g_post_ffn=g_post_ffn[:, None],
        wqkv=wqkv, wrest=wrest, wf=wf,
        b_f=jnp.pad(b_f, ((0, 0), (0, LANES - heads)))[:, None, :],
        bre=_block_diag(jnp.swapaxes(ssm_b_re, 2, 3), per).astype(BF16),
        bim=_block_diag(jnp.swapaxes(ssm_b_im, 2, 3), per).astype(BF16),
        cre=_block_diag(jnp.swapaxes(ssm_c_re, 2, 3), per).astype(BF16),
        cim=_block_diag(jnp.swapaxes(ssm_c_im, 2, 3), per).astype(BF16),
        ssm_d=ssm_d[:, None], w_glu=w_glu.astype(BF16),
        ab_re=ab_re[:, None], ab_im=ab_im[:, None], z_re=z_re[:, None], z_im=z_im[:, None],
        pool_w=pool_w.astype(BF16), pool_scale=pool_scale[:, None],
        w_branch=w_branch.astype(BF16), w_out=w_out.astype(BF16),
        w_up=w_up, conv_w=conv_w, conv_b=conv_b[:, None], w_down=w_down.astype(BF16))

    n_c = bp + bs
    c_rows = jnp.concatenate([c_prompt, c_sample, jnp.zeros((-n_c % 16, d), F32)], axis=0).astype(BF16)
    mod_all = _ada(c_rows, w_ada, b_ada)

    cache_lf = cache_logf.reshape(depth, n_pool, 1, psz * heads)
    pages = next(p for p in (16, 8, 4, 2, 1) if page_table.shape[1] % p == 0)

    def to_tm(a):
        return jnp.swapaxes(a.reshape((bs, ds) + a.shape[1:]), 0, 1).reshape((ts,) + a.shape[1:])

    def from_tm(a):
        return jnp.swapaxes(a.reshape((ds, bs) + a.shape[1:]), 0, 1)

    xp = x_prompt.reshape(tp, d)
    xs = to_tm(x_sample.reshape(ts, d))
    kv_p = None
    outs_p, outs_s = [], []
    for l in range(depth):
        mod_p = jnp.swapaxes(mod_all[l, :bp].reshape(bp, 6, 1, d), 0, 1)

        def attn_p(qt, k, vt, lf):
            return _attn_prompt(qt, k, vt, _fcum(lf, seq, heads), nb=bp, seq=seq, heads=heads, hd=hd,
                                tq=_pick(seq, 512), hpb=4)

        zst = jnp.zeros((bp, groups * nstate), F32)
        xp, (_, _, kv_p), lf, h_re, h_im, new_pool, new_conv = _layer(
            xp, mod_p, lw, l, attn_p, kv_p, zst, zst, jnp.zeros((bp, POOL_BUF, pw), F32),
            jnp.zeros((bp, CONV_BUF, 2 * f), F32),
            nb=bp, seq=seq, time_major=False, pos0=0, dims=dims)
        outs_p.append((lf[:, :heads].reshape(bp, seq, heads),
                       h_re.reshape(bp, groups, nstate), h_im.reshape(bp, groups, nstate),
                       new_pool, new_conv))

        mod_s = jnp.tile(mod_all[l, bp:n_c], (ds, 1)).reshape(ts, 6, d).transpose(1, 0, 2)[:, None]

        def attn_s(q, k, v, lf, l=l):
            lf_new = from_tm(lf[:, :heads]).reshape(bs, 1, ds * heads)
            lf_new = jnp.pad(lf_new, ((0, 0), (0, 0), (0, (psz - ds) * heads)))
            o = _attn_sample(page_table, from_tm(q), from_tm(k).reshape(bs, ds, heads, hd),
                             from_tm(v).reshape(bs, ds, heads, hd), lf_new,
                             cache_k, cache_v, cache_lf, l, pages=pages)
            return to_tm(o.reshape(ts, aw)).astype(BF16)

        pool_buf_s = jnp.swapaxes(state_pool[l], 0, 1).reshape(1, POOL_BUF * bs, pw)
        conv_buf_s = jnp.swapaxes(state_ffn_conv[l], 0, 1).reshape(1, CONV_BUF * bs, 2 * f)
        xs, (k, v, _), lf, h_re, h_im, new_pool, new_conv = _layer(
            xs, mod_s, lw, l, attn_s, None,
            state_ssm_re[l].reshape(bs, groups * nstate), state_ssm_im[l].reshape(bs, groups * nstate),
            pool_buf_s, conv_buf_s, nb=bs, seq=ds, time_major=True, pos0=past, dims=dims)
        outs_s.append((from_tm(k).reshape(bs, ds, heads, hd), from_tm(v).reshape(bs, ds, heads, hd),
                       from_tm(lf[:, :heads]),
                       h_re.reshape(bs, groups, nstate), h_im.reshape(bs, groups, nstate),
                       jnp.swapaxes(new_pool.reshape(POOL_BUF, bs, pw), 0, 1),
                       jnp.swapaxes(new_conv.reshape(CONV_BUF, bs, 2 * f), 0, 1)))

    stack = lambda outs: tuple(jnp.stack([o[i] for o in outs]) for i in range(len(outs[0])))
    return ((xp.reshape(bp, seq, d), from_tm(xs)) + kv_p + stack(outs_p) + stack(outs_s))
```

```python
import functools
import math

import jax
import jax.numpy as jnp
from jax import lax
from jax.experimental import pallas as pl
from jax.experimental.pallas import tpu as pltpu

F32 = jnp.float32
BF16 = jnp.bfloat16

EPS = 1e-6
POOL_WINDOWS = (2, 4, 8, 16)
POOL_BUF = max(POOL_WINDOWS) - 1
POOL_HDR = POOL_BUF + 1
CONV_WIDTH = 3
CONV_BUF = CONV_WIDTH - 1
CONV_HDR = 16
SUBLANES = 8
LANES = 128
MIB = 1024 * 1024
NEG_BIG = -1e30
LOG2E = math.log2(math.e)


def _cparams(semantics, vmem_mib):
    return pltpu.CompilerParams(dimension_semantics=semantics, vmem_limit_bytes=vmem_mib * MIB)


def _resident(shape):
    nd = len(shape)
    return pl.BlockSpec(shape, lambda *_: (0,) * nd, pipeline_mode=pl.Buffered(1))


def _layer_resident(arr, layer):
    nd = arr.ndim - 1
    return pl.BlockSpec((None,) + arr.shape[1:], lambda *_: (layer,) + (0,) * nd, pipeline_mode=pl.Buffered(1))


def _dot(a, b):
    return jnp.dot(a, b, preferred_element_type=F32)


def _dot_nt(a, b):
    return lax.dot_general(a, b, (((1,), (1,)), ((), ())), preferred_element_type=F32)


def _rms(x, g):
    return x * lax.rsqrt(jnp.mean(x * x, axis=-1, keepdims=True) + EPS) * g


def _gelu_tanh(x):
    c = math.sqrt(2.0 / math.pi)
    hx = 0.5 * x
    return hx + hx * jnp.tanh(x * (c + (c * 0.044715) * (x * x)))


def _log_sigmoid(x):
    return jnp.minimum(x, 0.0) - jnp.log1p(jnp.exp(-jnp.abs(x)))


def _div_pow2(x, c):
    assert c > 0 and c & (c - 1) == 0
    return x >> (c.bit_length() - 1)


def _row_cumsum(x):
    n = x.shape[0]
    row = lax.broadcasted_iota(jnp.int32, x.shape, 0)
    k = 1
    while k < n:
        x = x + jnp.where(row >= k, pltpu.roll(x, k, axis=0), 0.0)
        k *= 2
    return x


def _ada_kernel(c_ref, w_ref, b_ref, o_ref):
    o_ref[...] = _dot(c_ref[...], w_ref[...].astype(BF16)) + b_ref[...]


def _ada(c_rows, w_ada, b_ada):
    depth, d, n = w_ada.shape
    r = c_rows.shape[0]
    tn = 1024
    return pl.pallas_call(
        _ada_kernel,
        grid=(depth, n // tn),
        in_specs=[pl.BlockSpec((r, d), lambda l, j: (0, 0)),
                  pl.BlockSpec((None, d, tn), lambda l, j: (l, 0, j)),
                  pl.BlockSpec((None, 1, tn), lambda l, j: (l, 0, j))],
        out_specs=pl.BlockSpec((None, r, tn), lambda l, j: (l, 0, j)),
        out_shape=jax.ShapeDtypeStruct((depth, r, n), F32),
        compiler_params=_cparams(("arbitrary", "arbitrary"), 40),
        name="ada",
    )(c_rows, w_ada, b_ada.reshape(depth, 1, n))


def _repack_kernel(w_ref, qkv_ref, f_ref, rest_ref, *, off_f, heads):
    qkv_ref[...] = w_ref[0:off_f, :].astype(BF16)
    f_ref[...] = w_ref[off_f:off_f + heads, :]
    rest_ref[...] = w_ref[off_f + heads:, :].astype(BF16)


def _repack(w_t, off_f, heads):
    depth, n, d = w_t.shape
    tc = _pick(d, 256)
    nrest = n - off_f - heads
    assert off_f % SUBLANES == 0 and heads % SUBLANES == 0
    return pl.pallas_call(
        functools.partial(_repack_kernel, off_f=off_f, heads=heads),
        grid=(depth, d // tc),
        in_specs=[pl.BlockSpec((None, n, tc), lambda l, i: (l, 0, i))],
        out_specs=[pl.BlockSpec((None, off_f, tc), lambda l, i: (l, 0, i)),
                   pl.BlockSpec((None, heads, tc), lambda l, i: (l, 0, i)),
                   pl.BlockSpec((None, nrest, tc), lambda l, i: (l, 0, i))],
        out_shape=[jax.ShapeDtypeStruct((depth, off_f, d), BF16), jax.ShapeDtypeStruct((depth, heads, d), F32),
                   jax.ShapeDtypeStruct((depth, nrest, d), BF16)],
        compiler_params=_cparams(("arbitrary", "arbitrary"), 48),
        name="repack",
    )(w_t)


def _ssm_disc_kernel(are_ref, aim_ref, ls_ref, abre_ref, abim_ref, zre_ref, zim_ref):
    a_re, a_im = are_ref[...], aim_ref[...]
    step = jnp.exp(ls_ref[...])
    mag = jnp.exp(step * a_re)
    ab_re = mag * jnp.cos(step * a_im)
    ab_im = mag * jnp.sin(step * a_im)
    den = a_re * a_re + a_im * a_im
    abre_ref[...] = ab_re
    abim_ref[...] = ab_im
    zre_ref[...] = ((ab_re - 1.0) * a_re + ab_im * a_im) / den
    zim_ref[...] = (ab_im * a_re - (ab_re - 1.0) * a_im) / den


def _ssm_disc(a_re, a_im, log_step):
    depth, g, p = a_re.shape
    n = g * p
    ls = jnp.broadcast_to(log_step[:, :, None], (depth, g, p)).reshape(depth, n)
    spec = pl.BlockSpec((depth, n), lambda: (0, 0))
    return pl.pallas_call(
        _ssm_disc_kernel,
        in_specs=[spec] * 3,
        out_specs=[spec] * 4,
        out_shape=[jax.ShapeDtypeStruct((depth, n), F32)] * 4,
        name="ssm_disc",
    )(a_re.reshape(depth, n), a_im.reshape(depth, n), ls)


def _inproj_kernel(x_ref, g_ref, sc_ref, sh_ref, wqt_ref, wvt_ref, wk_ref, wsp_ref, wf_ref, bf_ref, *rest,
                   aw, sw, pw, hd, transposed, c2):
    h = _rms(x_ref[...], g_ref[...]) * (1.0 + sc_ref[...]) + sh_ref[...]
    hb = h.astype(BF16)
    k = _dot_nt(hb, wk_ref[...])
    v = _dot_nt(hb, wvt_ref[...])
    if transposed:
        h_ref, q_ref, kb_ref, vt_ref, us_ref, up_ref, lf_ref, k5_ref, v5_ref = rest[-9:]
        q_ref[...] = (_dot_nt(wqt_ref[...], hb) * c2).astype(q_ref.dtype)
        vt_ref[...] = _dot_nt(wvt_ref[...], hb).astype(vt_ref.dtype)
        kb_ref[...] = k.astype(kb_ref.dtype)
        k5_ref[...] = pltpu.einshape("htd->thd", jnp.stack([k[:, hh * hd:(hh + 1) * hd] for hh in range(aw // hd)]))
        v5_ref[...] = pltpu.einshape("htd->thd", jnp.stack([v[:, hh * hd:(hh + 1) * hd] for hh in range(aw // hd)]))
    else:
        h_ref, q_ref, k_ref, v_ref, us_ref, up_ref, lf_ref = rest
        q_ref[...] = _dot_nt(hb, wqt_ref[...])
        k_ref[...] = k
        v_ref[...] = v
    h_ref[...] = hb
    us_ref[...] = _dot_nt(hb, wsp_ref[0:sw, :])
    up_ref[...] = _dot_nt(hb, wsp_ref[sw:sw + pw, :])
    lf_ref[...] = _log_sigmoid(_dot(hb, wf_ref[...]) + bf_ref[...])


def _inproj(x, mod, lw, layer, kv_out, *, tm, seq, nb, sw, pw, hd, time_major, c2):
    t, d = x.shape
    wqkv, wrest, wf, bf = lw["wqkv"], lw["wrest"], lw["wf"], lw["b_f"]
    aw = wqkv.shape[1] // 3
    heads = aw // hd
    rb = mod.shape[2]
    row = lambda w: pl.BlockSpec((tm, w), lambda i: (i, 0))
    colT = pl.BlockSpec((aw, tm), lambda i: (0, i))
    common_in = [row(d), _layer_resident(lw["g_pre_mix"], layer)]
    once = dict(pipeline_mode=pl.Buffered(1))
    qkv_rows = lambda part: pl.BlockSpec((None, aw, d), lambda *_: (layer, part, 0), **once)
    weights = [qkv_rows(0), qkv_rows(2), qkv_rows(1),
               pl.BlockSpec((None, sw + pw, d), lambda *_: (layer, 0, 0), **once),
               _layer_resident(wf, layer), _layer_resident(bf, layer)]
    sd = jax.ShapeDtypeStruct
    if time_major:
        assert tm == t
        modspec = lambda kk: pl.BlockSpec((None, None, rb, d), lambda i: (kk, 0, 0, 0))
        return pl.pallas_call(
            functools.partial(_inproj_kernel, aw=aw, sw=sw, pw=pw, hd=hd, transposed=False, c2=c2),
            grid=(1,),
            in_specs=common_in + [modspec(1), modspec(0)] + weights,
            out_specs=[row(d), row(aw), row(aw), row(aw), row(sw), row(pw), row(LANES)],
            out_shape=[sd((t, d), BF16), sd((t, aw), F32), sd((t, aw), F32), sd((t, aw), F32),
                       sd((t, sw), F32), sd((t, pw), F32), sd((t, LANES), F32)],
            compiler_params=_cparams(("arbitrary",), 48),
            name="inproj",
        )(x, lw["g_pre_mix"], mod, mod, wqkv, wqkv, wqkv, wrest, wf, bf)
    tpb = seq // tm
    modspec = lambda kk: pl.BlockSpec((None, None, rb, d), lambda i: (kk, i // tpb, 0, 0))
    slab = pl.BlockSpec((None, None, tm, heads, hd), lambda i: (layer, i // tpb, i % tpb, 0, 0))
    kv_shape = (wqkv.shape[0], nb, seq, heads, hd)
    carried = [] if kv_out is None else list(kv_out)
    n_in = 2 + 2 + len(weights)
    outs = pl.pallas_call(
        functools.partial(_inproj_kernel, aw=aw, sw=sw, pw=pw, hd=hd, transposed=True, c2=c2),
        grid=(t // tm,),
        in_specs=common_in + [modspec(1), modspec(0)] + weights + [pl.BlockSpec(memory_space=pl.ANY)] * len(carried),
        out_specs=[row(d), colT, row(aw), colT, pl.BlockSpec((tm, sw), lambda i: (i % tpb, i // tpb)),
                   row(pw), row(LANES), slab, slab],
        out_shape=[sd((t, d), BF16), sd((aw, t), BF16), sd((t, aw), BF16), sd((aw, t), BF16),
                   sd((seq, nb * sw), F32), sd((t, pw), F32), sd((t, LANES), F32),
                   sd(kv_shape, F32), sd(kv_shape, F32)],
        input_output_aliases={n_in + c: 7 + c for c in range(len(carried))},
        compiler_params=_cparams(("arbitrary",), 48),
        name="inproj",
    )(x, lw["g_pre_mix"], mod, mod, wqkv, wqkv, wqkv, wrest, wf, bf, *carried)
    return outs[:7], (outs[7], outs[8])


def _gates_kernel(h_ref, w_ref, o_ref):
    o_ref[...] = jax.nn.sigmoid(_dot_nt(h_ref[...], w_ref[...]))


def _gates(h, wrest, layer, *, tm, skip):
    t, d = h.shape
    n = wrest.shape[1] - skip
    tn = 1024
    assert skip % tn == 0
    return pl.pallas_call(
        _gates_kernel,
        grid=(n // tn, t // tm),
        in_specs=[pl.BlockSpec((tm, d), lambda j, i: (i, 0)),
                  pl.BlockSpec((None, tn, d), lambda j, i: (layer, skip // tn + j, 0))],
        out_specs=pl.BlockSpec((tm, tn), lambda j, i: (i, j)),
        out_shape=jax.ShapeDtypeStruct((t, n), F32),
        compiler_params=_cparams(("arbitrary", "arbitrary"), 48),
        name="gates",
    )(h, wrest)


def _fcum_kernel(lf_ref, o_ref, *, heads):
    neg = _row_cumsum(lf_ref[...]) * (-LOG2E)
    hi = neg.astype(BF16).astype(F32)
    r1 = neg - hi
    mid = r1.astype(BF16).astype(F32)
    lo = (r1 - mid).astype(BF16).astype(F32)
    lane = lax.broadcasted_iota(jnp.int32, neg.shape, 1)
    out = jnp.where(lane < heads, hi,
                    jnp.where(lane < 2 * heads, pltpu.roll(mid, heads, axis=1),
                              jnp.where(lane < 3 * heads, pltpu.roll(lo, 2 * heads, axis=1), 0.0)))
    o_ref[...] = out.astype(o_ref.dtype)


def _fcum(lf, seq, heads):
    t, w = lf.shape
    return pl.pallas_call(
        functools.partial(_fcum_kernel, heads=heads),
        grid=(t // seq,),
        in_specs=[pl.BlockSpec((seq, w), lambda b: (b, 0))],
        out_specs=pl.BlockSpec((seq, w), lambda b: (b, 0)),
        out_shape=jax.ShapeDtypeStruct((t, w), BF16),
        name="fcum",
    )(lf)


def _attn_kernel(qt_ref, k_ref, vt_ref, f_ref, o_ref, m_sc, l_sc, acc_sc, *, hd, hpb, heads):
    i, j = pl.program_id(2), pl.program_id(3)
    tq = qt_ref.shape[1]

    @pl.when(j == 0)
    def _():
        m_sc[...] = jnp.full(m_sc.shape, NEG_BIG, F32)
        l_sc[...] = jnp.zeros(l_sc.shape, F32)
        acc_sc[...] = jnp.zeros(acc_sc.shape, F32)

    def block(diagonal):
        for g in range(hpb):
            rows = slice(g * hd, (g + 1) * hd)
            head = pl.program_id(1) * hpb + g
            r = lax.broadcasted_iota(jnp.int32, (LANES, tq), 0)
            pick = jnp.where(r < 3 * heads, jnp.where((r & (heads - 1)) == head, 1.0, 0.0), 0.0).astype(BF16)
            qaug = jnp.concatenate([qt_ref[rows, :], pick], axis=0)
            kaug = jnp.concatenate([k_ref[:, rows], f_ref[...]], axis=1)
            t = _dot(kaug, qaug)
            if diagonal:
                kpos = lax.broadcasted_iota(jnp.int32, t.shape, 0)
                qpos = lax.broadcasted_iota(jnp.int32, t.shape, 1)
                t = jnp.where(kpos <= qpos, t, NEG_BIG)
            m_old = m_sc[g]
            m_new = jnp.maximum(m_old, jnp.max(t, axis=0, keepdims=True))
            alpha = jnp.exp2(m_old - m_new)
            p = jnp.exp2(t - m_new)
            l_sc[g] = alpha * l_sc[g] + jnp.sum(p, axis=0, keepdims=True)
            acc_sc[rows, :] = alpha * acc_sc[rows, :] + _dot(vt_ref[rows, :], p.astype(BF16))
            m_sc[g] = m_new

    @pl.when(j < i)
    def _():
        block(False)

    @pl.when(j == i)
    def _():
        block(True)
        for g in range(hpb):
            rows = slice(g * hd, (g + 1) * hd)
            o_ref[:, rows] = (acc_sc[rows, :] / l_sc[g]).T.astype(o_ref.dtype)


def _attn_prompt(qt, k, vt, faug, *, nb, seq, heads, hd, tq, hpb):
    t = k.shape[0]
    nq = seq // tq
    w = hpb * hd
    assert heads & (heads - 1) == 0 and 3 * heads <= LANES
    kblk = lambda b, i, j: b * nq + jnp.minimum(j, i)
    return pl.pallas_call(
        functools.partial(_attn_kernel, hd=hd, hpb=hpb, heads=heads),
        grid=(nb, heads // hpb, nq, nq),
        in_specs=[pl.BlockSpec((w, tq), lambda b, h, i, j: (h, b * nq + i)),
                  pl.BlockSpec((tq, w), lambda b, h, i, j: (kblk(b, i, j), h)),
                  pl.BlockSpec((w, tq), lambda b, h, i, j: (h, kblk(b, i, j))),
                  pl.BlockSpec((tq, LANES), lambda b, h, i, j: (kblk(b, i, j), 0))],
        out_specs=pl.BlockSpec((tq, w), lambda b, h, i, j: (b * nq + i, h)),
        out_shape=jax.ShapeDtypeStruct((t, heads * hd), BF16),
        scratch_shapes=[pltpu.VMEM((hpb, 1, tq), F32), pltpu.VMEM((hpb, 1, tq), F32),
                        pltpu.VMEM((w, tq), F32)],
        compiler_params=_cparams(("arbitrary",) * 4, 40),
        name="attn_prompt",
    )(qt, k, vt, faug)


def _attn_sample_kernel(pt_ref, q_ref, kn_ref, vn_ref, lfn_ref, *rest, pages, heads, hd, c2):
    k_refs, v_refs, lf_refs = rest[0:pages], rest[pages:2 * pages], rest[2 * pages:3 * pages]
    o_ref, q2_sc, m_sc, l_sc, acc_sc, fc_sc = rest[3 * pages:]
    del pt_ref
    j = pl.program_id(1)
    nq = q_ref.shape[0]
    rows = heads * nq
    psz = k_refs[0].shape[0]
    cols = psz * heads
    assert nq & (nq - 1) == 0 and heads & (heads - 1) == 0

    @pl.when(j == 0)
    def _():
        q2 = jnp.concatenate([q_ref[:, h * hd:(h + 1) * hd] for h in range(heads)], axis=0)
        q2_sc[...] = (q2 * c2).astype(BF16)
        m_sc[...] = jnp.full(m_sc.shape, NEG_BIG, F32)
        l_sc[...] = jnp.zeros(l_sc.shape, F32)
        acc_sc[...] = jnp.zeros(acc_sc.shape, F32)
        fc_sc[...] = jnp.zeros(fc_sc.shape, F32)

    q2 = q2_sc[...]
    row = lax.broadcasted_iota(jnp.int32, (rows, cols), 0)
    col = lax.broadcasted_iota(jnp.int32, (rows, cols), 1)
    own_head = _div_pow2(row, nq) == (col & (heads - 1))

    def head_scan(x, inclusive_prefix):
        lane = lax.broadcasted_iota(jnp.int32, x.shape, 1)
        k = heads
        while k < cols:
            shifted = pltpu.roll(x, k, axis=1)
            x = x + (jnp.where(lane >= k, shifted, 0.0) if inclusive_prefix else shifted)
            k *= 2
        return x

    def attend(pages_kvl, causal=None):
        n = len(pages_kvl)
        lf = jnp.concatenate([x[2] for x in pages_kvl], axis=0) if n > 1 else pages_kvl[0][2]
        cum, tot = head_scan(lf, True), head_scan(lf, False)
        base = fc_sc[...]
        ts = []
        for c, (k3, _, _) in enumerate(pages_kvl):
            f2 = (base + cum[c:c + 1]) * LOG2E
            base = base + tot[c:c + 1]
            t = jnp.where(own_head, _dot_nt(q2, k3.reshape(cols, hd).astype(BF16)) - f2, NEG_BIG)
            if causal is not None:
                t = jnp.where(causal, t, NEG_BIG)
            ts.append(t)
        fc_sc[...] = base
        m_old = m_sc[...]
        m_new = m_old
        for t in ts:
            m_new = jnp.maximum(m_new, jnp.max(t, axis=-1, keepdims=True))
        alpha = jnp.exp2(m_old - m_new)
        l_new = alpha * l_sc[...]
        acc = alpha * acc_sc[...]
        for t, (_, v3, _) in zip(ts, pages_kvl):
            p = jnp.exp2(t - m_new)
            l_new = l_new + jnp.sum(p, axis=-1, keepdims=True)
            acc = acc + _dot(p.astype(BF16), v3.reshape(cols, hd).astype(BF16))
        l_sc[...] = l_new
        acc_sc[...] = acc
        m_sc[...] = m_new

    attend([(k_refs[c][...], v_refs[c][...], lf_refs[c][...]) for c in range(pages)])

    @pl.when(j == pl.num_programs(1) - 1)
    def _():
        zeros = jnp.zeros((psz - nq, heads, hd), F32)
        causal = _div_pow2(col, heads) <= (row & (nq - 1))
        attend([(jnp.concatenate([kn_ref[...], zeros], axis=0), jnp.concatenate([vn_ref[...], zeros], axis=0),
                 lfn_ref[...])], causal)
        o = acc_sc[...] / l_sc[...]
        for h in range(heads):
            o_ref[:, h * hd:(h + 1) * hd] = o[h * nq:(h + 1) * nq].astype(o_ref.dtype)


def _attn_sample(page_table, q, k_new, v_new, lf_new, cache_k, cache_v, cache_lf, layer, *, pages):
    nb, nq, width = q.shape
    _, _, psz, heads, hd = cache_k.shape
    cols = psz * heads
    steps = page_table.shape[1] // pages
    rows = heads * nq
    per_seq = lambda shape: pl.BlockSpec((None,) + shape, lambda b, j, pt: (b,) + (0,) * len(shape))

    def page_spec(shape, c):
        return pl.BlockSpec((None, None) + shape,
                            lambda b, j, pt: (layer, pt[b, j * pages + c]) + (0,) * len(shape))

    in_specs = ([per_seq((nq, width)), per_seq((nq, heads, hd)), per_seq((nq, heads, hd)), per_seq((1, cols))]
                + [page_spec((psz, heads, hd), c) for c in range(pages)]
                + [page_spec((psz, heads, hd), c) for c in range(pages)]
                + [page_spec((1, cols), c) for c in range(pages)])
    return pl.pallas_call(
        functools.partial(_attn_sample_kernel, pages=pages, heads=heads, hd=hd, c2=hd ** -0.5 * LOG2E),
        grid_spec=pltpu.PrefetchScalarGridSpec(
            num_scalar_prefetch=1,
            grid=(nb, steps),
            in_specs=in_specs,
            out_specs=per_seq((nq, width)),
            scratch_shapes=[pltpu.VMEM((rows, hd), BF16), pltpu.VMEM((rows, 1), F32),
                            pltpu.VMEM((rows, 1), F32), pltpu.VMEM((rows, hd), F32),
                            pltpu.VMEM((1, cols), F32)]),
        out_shape=jax.ShapeDtypeStruct((nb, nq, width), F32),
        compiler_params=_cparams(("arbitrary", "arbitrary"), 52),
        name="attn_sample",
    )(page_table, q, k_new, v_new, lf_new, *([cache_k] * pages), *([cache_v] * pages), *([cache_lf] * pages))


def _ssm_kernel(u_ref, bre_ref, bim_ref, cre_ref, cim_ref, d_ref, wglu_ref, abre_ref, abim_ref,
                zre_ref, zim_ref, h0re_ref, h0im_ref, o_ref, hre_ref, him_ref,
                sre, sim, cr, ci, *, nb, lane_chunk, fold):
    c = pl.program_id(0)
    tr, n = sre.shape
    half = SUBLANES // 2

    @pl.when(c == 0)
    def _():
        cr[...] = h0re_ref[...]
        ci[...] = h0im_ref[...]

    u = u_ref[...]
    if fold > 1:
        u = pltpu.einshape("l(bc)->(lb)c", u, b=fold)
    ub = u.astype(BF16)
    w = u.shape[1]
    nc = LANES * (n // w)
    blocks = [(kb, slice(kb * LANES, (kb + 1) * LANES), slice(kb * nc, (kb + 1) * nc)) for kb in range(w // LANES)]
    for kb, ch, st in blocks:
        bu_re, bu_im = _dot(ub[:, ch], bre_ref[kb]), _dot(ub[:, ch], bim_ref[kb])
        z_re, z_im = zre_ref[:, st], zim_ref[:, st]
        sre[:, st] = z_re * bu_re - z_im * bu_im
        sim[:, st] = z_re * bu_im + z_im * bu_re

    for lc in range(n // lane_chunk):
        lanes = slice(lc * lane_chunk, (lc + 1) * lane_chunk)
        a_re = jnp.broadcast_to(abre_ref[:, lanes], (SUBLANES, lane_chunk))
        a_im = jnp.broadcast_to(abim_ref[:, lanes], (SUBLANES, lane_chunk))
        lower = lax.broadcasted_iota(jnp.int32, (SUBLANES, lane_chunk), 0) < half

        def step(hr, hi, xr, xi):
            return a_re * hr - a_im * hi + xr, a_re * hi + a_im * hr + xi

        def body(r, carry):
            hr, hi = carry
            row = pl.multiple_of(r * SUBLANES, SUBLANES)
            xr, xi = sre[pl.ds(row, SUBLANES), lanes], sim[pl.ds(row, SUBLANES), lanes]
            if nb == SUBLANES:
                nr, ni = step(hr, hi, xr, xi)
            else:
                t1r, t1i = step(pltpu.roll(hr, half, axis=0), pltpu.roll(hi, half, axis=0), xr, xi)
                t2r, t2i = step(pltpu.roll(t1r, half, axis=0), pltpu.roll(t1i, half, axis=0), xr, xi)
                nr, ni = jnp.where(lower, t1r, t2r), jnp.where(lower, t1i, t2i)
            sre[pl.ds(row, SUBLANES), lanes] = nr
            sim[pl.ds(row, SUBLANES), lanes] = ni
            return nr, ni

        hr, hi = lax.fori_loop(0, tr // SUBLANES, body, (cr[:, lanes], ci[:, lanes]))
        cr[:, lanes] = hr
        ci[:, lanes] = hi

    y = jnp.concatenate([_dot(sre[:, st].astype(BF16), cre_ref[kb]) - _dot(sim[:, st].astype(BF16), cim_ref[kb])
                         for kb, _, st in blocks], axis=1) + d_ref[...] * u
    y = _gelu_tanh(y)
    o = y * jax.nn.sigmoid(_dot(y.astype(BF16), wglu_ref[...]))
    if fold > 1:
        o = pltpu.einshape("(lb)c->l(bc)", o, b=fold)
    o_ref[...] = o.astype(o_ref.dtype)

    @pl.when(c == pl.num_programs(0) - 1)
    def _():
        hre_ref[...] = cr[...]
        him_ref[...] = ci[...]


def _ssm(u, lw, layer, h0_re, h0_im, *, nb, tr, fold):
    rows, w = u.shape[0] * fold, u.shape[1] // fold
    assert w % LANES == 0 and fold in (1, nb)
    names = ("bre", "bim", "cre", "cim", "ssm_d", "w_glu", "ab_re", "ab_im", "z_re", "z_im")
    n = lw["ab_re"].shape[2]
    assert nb in (SUBLANES // 2, SUBLANES)
    lead = jnp.zeros((SUBLANES - nb, n), F32)
    h0_re, h0_im = jnp.concatenate([lead, h0_re], axis=0), jnp.concatenate([lead, h0_im], axis=0)
    o, h_re, h_im = pl.pallas_call(
        functools.partial(_ssm_kernel, nb=nb, lane_chunk=512, fold=fold),
        grid=(rows // tr,),
        in_specs=[pl.BlockSpec((tr // fold, w * fold), lambda c: (c, 0)),
                  *[_layer_resident(lw[k], layer) for k in names],
                  _resident(h0_re.shape), _resident(h0_im.shape)],
        out_specs=[pl.BlockSpec((tr // fold, w * fold), lambda c: (c, 0)),
                   pl.BlockSpec((SUBLANES, n), lambda c: (0, 0)), pl.BlockSpec((SUBLANES, n), lambda c: (0, 0))],
        out_shape=[jax.ShapeDtypeStruct(u.shape, BF16),
                   jax.ShapeDtypeStruct((SUBLANES, n), F32), jax.ShapeDtypeStruct((SUBLANES, n), F32)],
        scratch_shapes=[pltpu.VMEM((tr, n), F32), pltpu.VMEM((tr, n), F32),
                        pltpu.VMEM((SUBLANES, n), F32), pltpu.VMEM((SUBLANES, n), F32)],
        compiler_params=_cparams(("arbitrary",), 48),
        name="ssm",
    )(u, *[lw[k] for k in names], h0_re, h0_im)
    return o, h_re[SUBLANES - nb:], h_im[SUBLANES - nb:]


def _pool_kernel(u_ref, buf_ref, w_ref, sc_ref, o_ref, nb_ref, ext, *, rs, pos0):
    n = u_ref.shape[0]
    hdr = POOL_HDR * rs
    gw = w_ref.shape[2]
    ext[0:rs, :] = jnp.zeros((rs, ext.shape[1]), F32)
    ext[rs:hdr, :] = buf_ref[...]
    ext[hdr:hdr + n, :] = u_ref[...]
    pos = pos0 + _div_pow2(lax.broadcasted_iota(jnp.int32, (n, gw), 0), rs)
    for gi, win in enumerate(POOL_WINDOWS):
        lanes = slice(gi * gw, (gi + 1) * gw)
        tot = ext[hdr:hdr + n, lanes]
        for back in range(1, win):
            tot = tot + ext[hdr - back * rs:hdr - back * rs + n, lanes]
        cnt = jnp.minimum(pos + 1, win).astype(F32)
        pooled = tot / cnt - u_ref[:, lanes]
        o_ref[:, lanes] = (_dot(pooled.astype(BF16), w_ref[gi]) * sc_ref[:, lanes]).astype(o_ref.dtype)
    nb_ref[...] = ext[n + rs:n + hdr, :]


def _pool(u, buf, w, scale, layer, *, blocks, rs, pos0):
    t, width = u.shape
    n = t // blocks
    hist = POOL_BUF * rs
    return pl.pallas_call(
        functools.partial(_pool_kernel, rs=rs, pos0=pos0),
        grid=(blocks,),
        in_specs=[pl.BlockSpec((n, width), lambda b: (b, 0)),
                  pl.BlockSpec((None, hist, width), lambda b: (b, 0, 0)),
                  _layer_resident(w, layer), _layer_resident(scale, layer)],
        out_specs=[pl.BlockSpec((n, width), lambda b: (b, 0)),
                   pl.BlockSpec((None, hist, width), lambda b: (b, 0, 0))],
        out_shape=[jax.ShapeDtypeStruct((t, width), BF16), jax.ShapeDtypeStruct((blocks, hist, width), F32)],
        scratch_shapes=[pltpu.VMEM((POOL_HDR * rs + n, width), F32)],
        compiler_params=_cparams(("arbitrary",), 48),
        name="pool",
    )(u, buf, w, scale)


def _merge_kernel(gt_ref, oa_ref, os_ref, op_ref, x_ref, gm_ref, gpost_ref, gpre_ref, scf_ref, shf_ref,
                  wb_ref, wo_ref, xo_ref, h2_ref, *, d, aw, sw):
    merged = (gt_ref[:, 0:d] * _dot(oa_ref[...], wb_ref[0:aw, :])
              + gt_ref[:, d:2 * d] * _dot(os_ref[...], wb_ref[aw:aw + sw, :])
              + gt_ref[:, 2 * d:3 * d] * _dot(op_ref[...], wb_ref[aw + sw:, :]))
    y = _dot(merged.astype(BF16), wo_ref[...])
    x = x_ref[...] + gm_ref[...] * _rms(y, gpost_ref[...])
    xo_ref[...] = x
    h2_ref[...] = (_rms(x, gpre_ref[...]) * (1.0 + scf_ref[...]) + shf_ref[...]).astype(h2_ref.dtype)


def _merge(gates, o_att, o_ssm, o_pool, x, mod, g_post, g_pre_ffn, wb, wo, layer, *, tm, seq, time_major,
           aw, sw):
    t, d = x.shape
    rb = mod.shape[2]
    if time_major:
        bidx = lambda i: 0
        os_spec = pl.BlockSpec((tm, sw), lambda i: (i, 0))
    else:
        tpb = seq // tm
        bidx = lambda i: i // tpb
        os_spec = pl.BlockSpec((tm, sw), lambda i: (i % tpb, i // tpb))
    row = lambda w: pl.BlockSpec((tm, w), lambda i: (i, 0))
    modspec = lambda kk: pl.BlockSpec((None, None, rb, d), lambda i: (kk, bidx(i), 0, 0))
    return pl.pallas_call(
        functools.partial(_merge_kernel, d=d, aw=aw, sw=sw),
        grid=(t // tm,),
        in_specs=[row(3 * d), row(aw), os_spec, row(o_pool.shape[1]), row(d),
                  modspec(2), _layer_resident(g_post, layer), _layer_resident(g_pre_ffn, layer),
                  modspec(4), modspec(3), _layer_resident(wb, layer), _layer_resident(wo, layer)],
        out_specs=[row(d), row(d)],
        out_shape=[jax.ShapeDtypeStruct((t, d), F32), jax.ShapeDtypeStruct((t, d), BF16)],
        compiler_params=_cparams(("arbitrary",), 56),
        name="merge",
    )(gates, o_att, o_ssm, o_pool, x, mod, g_post, g_pre_ffn, mod, mod, wb, wo)


def _ffn_up_kernel(h_ref, halo_ref, wg_ref, wv_ref, cwg_ref, cwv_ref, cbg_ref, cbv_ref, bg_ref, bv_ref,
                   act_ref, ncg_ref, ncv_ref, ext, wb, *, rs, tpb):
    i = pl.program_id(1)
    tm = h_ref.shape[0]
    tn = wg_ref.shape[1]
    back = CONV_BUF * rs

    @pl.when(i == 0)
    def _():
        wb[:, 0:tn] = wg_ref[...].astype(BF16)
        wb[:, tn:2 * tn] = wv_ref[...].astype(BF16)

    if rs == 1:
        ext[CONV_HDR - SUBLANES:CONV_HDR, :] = _dot(halo_ref[...], wb[...])

        @pl.when(i % tpb == 0)
        def _():
            ext[CONV_HDR - back:CONV_HDR, 0:tn] = bg_ref[...]
            ext[CONV_HDR - back:CONV_HDR, tn:2 * tn] = bv_ref[...]
    else:
        ext[CONV_HDR - back:CONV_HDR, 0:tn] = bg_ref[...]
        ext[CONV_HDR - back:CONV_HDR, tn:2 * tn] = bv_ref[...]

    cw = jnp.concatenate([cwg_ref[...], cwv_ref[...]], axis=1)
    cb = jnp.concatenate([cbg_ref[...], cbv_ref[...]], axis=1)
    up = _dot(h_ref[...], wb[...])
    ext[CONV_HDR:CONV_HDR + tm, :] = up
    y = cb + cw[CONV_WIDTH - 1:CONV_WIDTH, :] * up
    for tap in range(CONV_WIDTH - 1):
        off = CONV_HDR - (CONV_WIDTH - 1 - tap) * rs
        y = y + cw[tap:tap + 1, :] * ext[off:off + tm, :]
    act_ref[...] = (_gelu_tanh(y[:, 0:tn]) * y[:, tn:2 * tn]).astype(act_ref.dtype)

    ncg_ref[...] = ext[CONV_HDR + tm - back:CONV_HDR + tm, 0:tn]
    ncv_ref[...] = ext[CONV_HDR + tm - back:CONV_HDR + tm, tn:2 * tn]


def _ffn_up(h2, w_up, conv_w, conv_b, layer, buf, *, tm, seq, rs, tn):
    t, d = h2.shape
    f = w_up.shape[2] // 2
    nj = f // tn
    back = CONV_BUF * rs
    nblk = buf.shape[0]
    tpb = max(seq // tm, 1) if rs == 1 else 1
    blk = (lambda i: i // tpb) if rs == 1 else (lambda i: 0)
    halo_rows = tm // SUBLANES
    col = lambda rows, off: pl.BlockSpec((None, rows, tn), lambda j, i: (layer, 0, j + off))
    state = lambda off: pl.BlockSpec((None, back, tn), lambda j, i: (blk(i), 0, j + off))
    return pl.pallas_call(
        functools.partial(_ffn_up_kernel, rs=rs, tpb=tpb),
        grid=(nj, t // tm),
        in_specs=[pl.BlockSpec((tm, d), lambda j, i: (i, 0)),
                  pl.BlockSpec((SUBLANES, d), lambda j, i: (jnp.maximum(i * halo_rows - 1, 0), 0)),
                  col(d, 0), col(d, nj), col(CONV_WIDTH, 0), col(CONV_WIDTH, nj), col(1, 0), col(1, nj),
                  state(0), state(nj)],
        out_specs=[pl.BlockSpec((tm, tn), lambda j, i: (i, j)),
                   pl.BlockSpec((None, back, tn), lambda j, i: (blk(i), 0, j)),
                   pl.BlockSpec((None, back, tn), lambda j, i: (blk(i), 0, j))],
        out_shape=[jax.ShapeDtypeStruct((t, f), BF16),
                   jax.ShapeDtypeStruct((nblk, back, f), F32), jax.ShapeDtypeStruct((nblk, back, f), F32)],
        scratch_shapes=[pltpu.VMEM((CONV_HDR + tm, 2 * tn), F32), pltpu.VMEM((d, 2 * tn), BF16)],
        compiler_params=_cparams(("arbitrary", "arbitrary"), 56),
        name="ffn_up",
    )(h2, h2, w_up, w_up, conv_w, conv_w, conv_b, conv_b, buf, buf)


def _ffn_down_kernel(act_ref, w_ref, x_ref, gf_ref, gpost_ref, o_ref):
    o_ref[...] = x_ref[...] + gf_ref[...] * _rms(_dot(act_ref[...], w_ref[...]), gpost_ref[...])


def _ffn_down(act, w_down, x, mod, g_post, layer, *, tm, seq, time_major):
    t, d = x.shape
    f = act.shape[1]
    rb = mod.shape[2]
    tpb = 1 if time_major else seq // tm
    bidx = (lambda i: 0) if time_major else (lambda i: i // tpb)
    return pl.pallas_call(
        _ffn_down_kernel,
        grid=(t // tm,),
        in_specs=[pl.BlockSpec((tm, f), lambda i: (i, 0)), _layer_resident(w_down, layer),
                  pl.BlockSpec((tm, d), lambda i: (i, 0)),
                  pl.BlockSpec((None, None, rb, d), lambda i: (5, bidx(i), 0, 0)),
                  _layer_resident(g_post, layer)],
        out_specs=pl.BlockSpec((tm, d), lambda i: (i, 0)),
        out_shape=jax.ShapeDtypeStruct((t, d), F32),
        compiler_params=_cparams(("arbitrary",), 56),
        name="ffn_down",
    )(act, w_down, x, mod, g_post)


def _block_diag(w, per):
    depth, g, r, c = w.shape
    eye = jnp.eye(per, dtype=w.dtype)
    w = w.reshape(depth, g // per, per, r, c)
    return (w[:, :, :, :, None, :] * eye[None, None, :, None, :, None]).reshape(depth, g // per, per * r, per * c)


def _pick(n, pref):
    return pref if n % pref == 0 else n


def _layer(x, mod, lw, layer, attn_fn, kv_out, h0_re, h0_im, pool_buf, conv_buf, *, nb, seq, time_major, pos0,
           dims):
    aw, sw, pw = dims["aw"], dims["sw"], dims["pw"]
    t, d = x.shape
    rs = nb if time_major else 1
    tm_in = t if time_major else _pick(seq, 256)
    res = _inproj(x, mod, lw, layer, kv_out, tm=tm_in, seq=seq, nb=nb, sw=sw, pw=pw,
                  hd=dims["hd"], time_major=time_major, c2=dims["c2"])
    (h, q, k, v, u_ssm, u_pool, lf), kv_out = res if not time_major else (res, None)
    gates = _gates(h, lw["wrest"], layer, tm=t if time_major else _pick(t, 1024), skip=sw + pw)
    o_att = attn_fn(q, k, v, lf)
    o_ssm, h_re, h_im = _ssm(u_ssm, lw, layer, h0_re, h0_im, nb=nb, tr=_pick(t, 512),
                             fold=1 if time_major else nb)
    o_pool, new_pool = _pool(u_pool, pool_buf, lw["pool_w"], lw["pool_scale"], layer,
                             blocks=1 if time_major else nb, rs=rs, pos0=pos0)
    x, h2 = _merge(gates, o_att, o_ssm, o_pool, x, mod, lw["g_post_mix"], lw["g_pre_ffn"], lw["w_branch"],
                   lw["w_out"], layer, tm=tm_in, seq=seq, time_major=time_major, aw=aw, sw=sw)
    act, ncg, ncv = _ffn_up(h2, lw["w_up"], lw["conv_w"], lw["conv_b"], layer, conv_buf,
                            tm=t if time_major else _pick(seq, 1024), seq=seq, rs=rs, tn=dims["tn_ff"])
    x = _ffn_down(act, lw["w_down"], x, mod, lw["g_post_ffn"], layer, tm=tm_in, seq=seq, time_major=time_major)
    return x, (k, v, kv_out), lf, h_re, h_im, new_pool, jnp.concatenate([ncg, ncv], axis=-1)


def kernel(x_prompt, x_sample, cache_k, cache_v, cache_logf, page_table, state_ssm_re, state_ssm_im, state_pool, state_ffn_conv, c_prompt, c_sample, w_ada, b_ada, g_pre_mix, g_post_mix, g_pre_ffn, g_post_ffn, w_in, b_f, ssm_a_re, ssm_a_im, ssm_log_step, ssm_b_re, ssm_b_im, ssm_c_re, ssm_c_im, ssm_d, w_glu, pool_w, pool_scale, w_branch, w_out, w_up, conv_w, conv_b, w_down):
    bp, seq, d = x_prompt.shape
    bs, ds, _ = x_sample.shape
    depth, n_pool, psz, heads, hd = cache_k.shape
    aw = heads * hd
    groups, nstate = ssm_a_re.shape[1], ssm_a_re.shape[2]
    sw = ssm_d.shape[1]
    pw = pool_scale.shape[1]
    f = w_down.shape[1]
    past = page_table.shape[1] * psz
    tp, ts = bp * seq, bs * ds
    off_f = 3 * aw
    dims = dict(aw=aw, sw=sw, pw=pw, hd=hd, tn_ff=_pick(f, 512), c2=hd ** -0.5 * LOG2E)
    ab_re, ab_im, z_re, z_im = _ssm_disc(ssm_a_re, ssm_a_im, ssm_log_step)
    wqkv, wf_t, wrest = _repack(jnp.swapaxes(w_in, 1, 2), off_f, heads)
    wf = jnp.pad(jnp.swapaxes(wf_t, 1, 2), ((0, 0), (0, 0), (0, LANES - heads))).astype(BF16)
    per = LANES // (sw // groups)
    lw = dict(
        g_pre_mix=g_pre_mix[:, None], g_post_mix=g_post_mix[:, None],
        g_pre_ffn=g_pre_ffn[:, None], g_post_ffn=g_post_ffn[:, None],
        wqkv=wqkv, wrest=wrest, wf=wf,
        b_f=jnp.pad(b_f, ((0, 0), (0, LANES - heads)))[:, None, :],
        bre=_block_diag(jnp.swapaxes(ssm_b_re, 2, 3), per).astype(BF16),
        bim=_block_diag(jnp.swapaxes(ssm_b_im, 2, 3), per).astype(BF16),
        cre=_block_diag(jnp.swapaxes(ssm_c_re, 2, 3), per).astype(BF16),
        cim=_block_diag(jnp.swapaxes(ssm_c_im, 2, 3), per).astype(BF16),
        ssm_d=ssm_d[:, None], w_glu=w_glu.astype(BF16),
        ab_re=ab_re[:, None], ab_im=ab_im[:, None], z_re=z_re[:, None], z_im=z_im[:, None],
        pool_w=pool_w.astype(BF16), pool_scale=pool_scale[:, None],
        w_branch=w_branch.astype(BF16), w_out=w_out.astype(BF16),
        w_up=w_up, conv_w=conv_w, conv_b=conv_b[:, None], w_down=w_down.astype(BF16))

    n_c = bp + bs
    c_rows = jnp.concatenate([c_prompt, c_sample, jnp.zeros((-n_c % 16, d), F32)], axis=0).astype(BF16)
    mod_all = _ada(c_rows, w_ada, b_ada)

    cache_lf = cache_logf.reshape(depth, n_pool, 1, psz * heads)
    pages = next(p for p in (16, 8, 4, 2, 1) if page_table.shape[1] % p == 0)

    def to_tm(a):
        return jnp.swapaxes(a.reshape((bs, ds) + a.shape[1:]), 0, 1).reshape((ts,) + a.shape[1:])

    def from_tm(a):
        return jnp.swapaxes(a.reshape((ds, bs) + a.shape[1:]), 0, 1)

    xp = x_prompt.reshape(tp, d)
    xs = to_tm(x_sample.reshape(ts, d))
    kv_p = None
    outs_p, outs_s = [], []
    for l in range(depth):
        mod_p = jnp.swapaxes(mod_all[l, :bp].reshape(bp, 6, 1, d), 0, 1)

        def attn_p(qt, k, vt, lf):
            return _attn_prompt(qt, k, vt, _fcum(lf, seq, heads), nb=bp, seq=seq, heads=heads, hd=hd,
                                tq=_pick(seq, 512), hpb=8)

        zst = jnp.zeros((bp, groups * nstate), F32)
        xp, (_, _, kv_p), lf, h_re, h_im, new_pool, new_conv = _layer(
            xp, mod_p, lw, l, attn_p, kv_p, zst, zst, jnp.zeros((bp, POOL_BUF, pw), F32),
            jnp.zeros((bp, CONV_BUF, 2 * f), F32),
            nb=bp, seq=seq, time_major=False, pos0=0, dims=dims)
        outs_p.append((lf[:, :heads].reshape(bp, seq, heads),
                       h_re.reshape(bp, groups, nstate), h_im.reshape(bp, groups, nstate),
                       new_pool, new_conv))

        mod_s = jnp.tile(mod_all[l, bp:n_c], (ds, 1)).reshape(ts, 6, d).transpose(1, 0, 2)[:, None]

        def attn_s(q, k, v, lf, l=l):
            lf_new = from_tm(lf[:, :heads]).reshape(bs, 1, ds * heads)
            lf_new = jnp.pad(lf_new, ((0, 0), (0, 0), (0, (psz - ds) * heads)))
            o = _attn_sample(page_table, from_tm(q), from_tm(k).reshape(bs, ds, heads, hd),
                             from_tm(v).reshape(bs, ds, heads, hd), lf_new,
                             cache_k, cache_v, cache_lf, l, pages=pages)
            return to_tm(o.reshape(ts, aw)).astype(BF16)

        pool_buf_s = jnp.swapaxes(state_pool[l], 0, 1).reshape(1, POOL_BUF * bs, pw)
        conv_buf_s = jnp.swapaxes(state_ffn_conv[l], 0, 1).reshape(1, CONV_BUF * bs, 2 * f)
        xs, (k, v, _), lf, h_re, h_im, new_pool, new_conv = _layer(
            xs, mod_s, lw, l, attn_s, None,
            state_ssm_re[l].reshape(bs, groups * nstate), state_ssm_im[l].reshape(bs, groups * nstate),
            pool_buf_s, conv_buf_s, nb=bs, seq=ds, time_major=True, pos0=past, dims=dims)
        outs_s.append((from_tm(k).reshape(bs, ds, heads, hd), from_tm(v).reshape(bs, ds, heads, hd),
                       from_tm(lf[:, :heads]),
                       h_re.reshape(bs, groups, nstate), h_im.reshape(bs, groups, nstate),
                       jnp.swapaxes(new_pool.reshape(POOL_BUF, bs, pw), 0, 1),
                       jnp.swapaxes(new_conv.reshape(CONV_BUF, bs, 2 * f), 0, 1)))

    stack = lambda outs: tuple(jnp.stack([o[i] for o in outs]) for i in range(len(outs[0])))
    return ((xp.reshape(bp, seq, d), from_tm(xs)) + kv_p + stack(outs_p) + stack(outs_s))
```
